```python
import jax, jax.numpy as jnp
from jax import lax
import numpy as np

D_MODEL = 1024
BATCH = 8
SEQ = 8192
DEPTH = 1

CHUNK = 64
Q_BLOCK = 128
D_MIX = D_MODEL
MLA_HEADS = 8
MLA_NOPE = 64
MLA_ROPE = 32
MLA_QK = MLA_NOPE + MLA_ROPE
MLA_V = 64
MLA_OUT = MLA_HEADS * MLA_V
Q_LORA = 256
KV_LORA = 128
ROPE_BASE = 10000.0
RNN_WIDTH = D_MIX - MLA_OUT
RNN_BLOCKS = 8
RNN_BLOCK_DIM = RNN_WIDTH // RNN_BLOCKS
CONV_WIDTH = 4
LRU_C = 8.0
MEM_TOKENS = 256
MEM_HEADS = 4
MEM_HEAD_DIM = D_MODEL // MEM_HEADS
N_GROUPS = 4
EXPERTS_PER_GROUP = 8
N_EXPERTS = N_GROUPS * EXPERTS_PER_GROUP
TOP_K = 2
D_EXPERT = 256
EPS = 1e-6
IN_SPLITS = (Q_LORA, Q_LORA + KV_LORA, Q_LORA + KV_LORA + MLA_ROPE,
             Q_LORA + KV_LORA + MLA_ROPE + RNN_WIDTH)
D_IN = Q_LORA + KV_LORA + MLA_ROPE + 2 * RNN_WIDTH

kernel_name = "hybrid_mla_rglru_hmoe_block"


def rms_norm(x, g):
    xf = x.astype(jnp.float32)
    y = xf * lax.rsqrt(jnp.mean(xf * xf, axis=-1, keepdims=True) + EPS)
    return (y * g.astype(jnp.float32)).astype(x.dtype)


def rope_tables(seq_len):
    pos = jnp.arange(seq_len, dtype=jnp.float32)
    inv_freq = ROPE_BASE ** (-jnp.arange(0, MLA_ROPE, 2, dtype=jnp.float32) / MLA_ROPE)
    ang = pos[:, None] * inv_freq[None, :]
    return jnp.cos(ang), jnp.sin(ang)


def apply_rope(x, cos, sin):
    half = x.shape[-1] // 2
    x1, x2 = x[..., :half], x[..., half:]
    c = cos[None, :, None, :].astype(x.dtype)
    s = sin[None, :, None, :].astype(x.dtype)
    return jnp.concatenate([x1 * c - x2 * s, x2 * c + x1 * s], axis=-1)


def block_causal_attention(q, k, v):
    B, S, H, Dq = q.shape
    nblk = S // Q_BLOCK
    scale = Dq ** -0.5
    key_chunk = jnp.arange(S) // CHUNK
    qb = q.reshape(B, nblk, Q_BLOCK, H, Dq).transpose(1, 0, 2, 3, 4)

    def one_block(args):
        qi, bi = args
        q_chunk = (bi * Q_BLOCK + jnp.arange(Q_BLOCK)) // CHUNK
        s = jnp.einsum('bqhd,bkhd->bhqk', qi, k, preferred_element_type=jnp.float32) * scale
        mask = key_chunk[None, :] <= q_chunk[:, None]
        s = jnp.where(mask[None, None], s, -jnp.inf)
        p = jax.nn.softmax(s, axis=-1).astype(v.dtype)
        return jnp.einsum('bhqk,bkhd->bqhd', p, v)

    o = lax.map(one_block, (qb, jnp.arange(nblk)))
    return o.transpose(1, 0, 2, 3, 4).reshape(B, S, H * v.shape[-1])


def rg_lru(u, conv_w, conv_b, w_rg, b_rg, w_ig, b_ig, lam):
    B, S, W = u.shape
    up = jnp.pad(u, ((0, 0), (CONV_WIDTH - 1, 0), (0, 0)))
    xc = sum(up[:, j:j + S] * conv_w[j] for j in range(CONV_WIDTH)) + conv_b
    xb = xc.reshape(B, S, RNN_BLOCKS, RNN_BLOCK_DIM)
    r = jax.nn.sigmoid(jnp.einsum('bsnc,ncd->bsnd', xb, w_rg).reshape(B, S, W) + b_rg)
    i = jax.nn.sigmoid(jnp.einsum('bsnc,ncd->bsnd', xb, w_ig).reshape(B, S, W) + b_ig)
    log_a = (-LRU_C * r.astype(jnp.float32)) * jax.nn.softplus(-lam.astype(jnp.float32))
    a = jnp.exp(log_a)
    bt = jnp.sqrt(-jnp.expm1(2.0 * log_a)) * (i * xc).astype(jnp.float32)

    def combine(left, right):
        a_l, b_l = left
        a_r, b_r = right
        return a_l * a_r, a_r * b_l + b_r

    _, h = lax.associative_scan(combine, (a, bt), axis=1)
    return h.astype(u.dtype)


def mem_cross_attention(h, m, w_mq, w_mk, w_mv, g_mqn, g_mkn, w_mo):
    B, S, _ = h.shape
    M = m.shape[1]
    q = rms_norm((h @ w_mq).reshape(B, S, MEM_HEADS, MEM_HEAD_DIM), g_mqn)
    k = rms_norm((m @ w_mk).reshape(B, M, MEM_HEADS, MEM_HEAD_DIM), g_mkn)
    v = (m @ w_mv).reshape(B, M, MEM_HEADS, MEM_HEAD_DIM)
    s = jnp.einsum('bshd,bmhd->bhsm', q, k, preferred_element_type=jnp.float32) * MEM_HEAD_DIM ** -0.5
    p = jax.nn.softmax(s, axis=-1).astype(v.dtype)
    o = jnp.einsum('bhsm,bmhd->bshd', p, v).reshape(B, S, MEM_HEADS * MEM_HEAD_DIM)
    return o @ w_mo


def hier_moe(h, w_group, b_group, w_expert, b_expert, w_e_gate, w_e_up, w_e_down):
    B, S, D = h.shape
    t = h.reshape(-1, D)
    p_group = jax.nn.softmax((t @ w_group).astype(jnp.float32) + b_group, axis=-1)
    g_idx = jnp.argmax(p_group, axis=-1)
    p_g = jnp.take_along_axis(p_group, g_idx[:, None], axis=-1)
    e_logits = ((t @ w_expert).astype(jnp.float32) + b_expert).reshape(-1, N_GROUPS, EXPERTS_PER_GROUP)
    e_sel = jnp.take_along_axis(e_logits, g_idx[:, None, None], axis=1)[:, 0]
    top_v, top_i = lax.top_k(jax.nn.softmax(e_sel, axis=-1), TOP_K)
    top_v = top_v / jnp.sum(top_v, axis=-1, keepdims=True)
    within = jnp.sum(jax.nn.one_hot(top_i, EXPERTS_PER_GROUP, dtype=jnp.float32) * top_v[..., None], axis=1)
    gates = (jax.nn.one_hot(g_idx, N_GROUPS, dtype=jnp.float32)[:, :, None]
             * (p_g * within)[:, None, :]).reshape(-1, N_EXPERTS).astype(t.dtype)
    y = jnp.zeros_like(t)
    for e in range(N_EXPERTS):
        he = jax.nn.silu(t @ w_e_gate[e]) * (t @ w_e_up[e])
        y = y + gates[:, e:e + 1] * (he @ w_e_down[e])
    return y.reshape(B, S, D)


def setup_inputs(seed: int = 0) -> dict:
    key = jax.random.key(seed)
    ks = iter(jax.random.split(key, 64))
    L = DEPTH
    f32 = jnp.float32

    def nrm(shape, fan_in):
        return jax.random.normal(next(ks), shape, f32) * fan_in ** -0.5

    def gain(shape):
        return 1.0 + 0.02 * jax.random.normal(next(ks), shape, f32)

    def bias(shape):
        return 0.01 * jax.random.normal(next(ks), shape, f32)

    x = jax.random.normal(next(ks), (BATCH, SEQ, D_MODEL), f32)
    mem = jax.random.normal(next(ks), (BATCH, MEM_TOKENS, D_MODEL), f32)
    a0 = jax.random.uniform(next(ks), (L, RNN_WIDTH), f32, minval=0.9, maxval=0.999)
    s0 = a0 ** (1.0 / LRU_C)
    lam = jnp.log(s0) - jnp.log1p(-s0)
    return {
        "x": x,
        "mem": mem,
        "g_mix": gain((L, D_MODEL)),
        "w_in": nrm((L, D_MODEL, D_IN), D_MODEL),
        "g_cq": gain((L, Q_LORA)),
        "w_uq": nrm((L, Q_LORA, MLA_HEADS * MLA_QK), Q_LORA),
        "g_ckv": gain((L, KV_LORA)),
        "w_ukv": nrm((L, KV_LORA, MLA_HEADS * (MLA_NOPE + MLA_V)), KV_LORA),
        "g_qn": gain((L, MLA_QK)),
        "g_kn": gain((L, MLA_QK)),
        "conv_w": nrm((L, CONV_WIDTH, RNN_WIDTH), CONV_WIDTH),
        "conv_b": bias((L, RNN_WIDTH)),
        "w_rg": nrm((L, RNN_BLOCKS, RNN_BLOCK_DIM, RNN_BLOCK_DIM), RNN_BLOCK_DIM),
        "b_rg": bias((L, RNN_WIDTH)),
        "w_ig": nrm((L, RNN_BLOCKS, RNN_BLOCK_DIM, RNN_BLOCK_DIM), RNN_BLOCK_DIM),
        "b_ig": bias((L, RNN_WIDTH)),
        "lam": lam,
        "g_attn_out": gain((L, MLA_OUT)),
        "g_rnn_out": gain((L, RNN_WIDTH)),
        "w_out": nrm((L, D_MIX, D_MODEL), D_MIX),
        "g_xq": gain((L, D_MODEL)),
        "g_mem": gain((L, D_MODEL)),
        "w_mq": nrm((L, D_MODEL, MEM_HEADS * MEM_HEAD_DIM), D_MODEL),
        "w_mk": nrm((L, D_MODEL, MEM_HEADS * MEM_HEAD_DIM), D_MODEL),
        "w_mv": nrm((L, D_MODEL, MEM_HEADS * MEM_HEAD_DIM), D_MODEL),
        "g_mqn": gain((L, MEM_HEAD_DIM)),
        "g_mkn": gain((L, MEM_HEAD_DIM)),
        "w_mo": nrm((L, MEM_HEADS * MEM_HEAD_DIM, D_MODEL), MEM_HEADS * MEM_HEAD_DIM),
        "g_ffn": gain((L, D_MODEL)),
        "w_group": nrm((L, D_MODEL, N_GROUPS), D_MODEL),
        "b_group": bias((L, N_GROUPS)),
        "w_expert": nrm((L, D_MODEL, N_EXPERTS), D_MODEL),
        "b_expert": bias((L, N_EXPERTS)),
        "w_e_gate": nrm((L, N_EXPERTS, D_MODEL, D_EXPERT), D_MODEL),
        "w_e_up": nrm((L, N_EXPERTS, D_MODEL, D_EXPERT), D_MODEL),
        "w_e_down": nrm((L, N_EXPERTS, D_EXPERT, D_MODEL), D_EXPERT),
    }


def reference(x, mem, g_mix, w_in, g_cq, w_uq, g_ckv, w_ukv, g_qn, g_kn,
              conv_w, conv_b, w_rg, b_rg, w_ig, b_ig, lam, g_attn_out, g_rnn_out, w_out,
              g_xq, g_mem, w_mq, w_mk, w_mv, g_mqn, g_mkn, w_mo,
              g_ffn, w_group, b_group, w_expert, b_expert, w_e_gate, w_e_up, w_e_down):
    B, S, _ = x.shape
    cos, sin = rope_tables(S)
    for l in range(DEPTH):
        h = rms_norm(x, g_mix[l])
        z = h @ w_in[l]
        cq, ckv, k_rope, u_gate, u_x = jnp.split(z, IN_SPLITS, axis=-1)
        q = (rms_norm(cq, g_cq[l]) @ w_uq[l]).reshape(B, S, MLA_HEADS, MLA_QK)
        kv = (rms_norm(ckv, g_ckv[l]) @ w_ukv[l]).reshape(B, S, MLA_HEADS, MLA_NOPE + MLA_V)
        k_nope, v = kv[..., :MLA_NOPE], kv[..., MLA_NOPE:]
        k = jnp.concatenate(
            [k_nope, jnp.broadcast_to(k_rope[:, :, None, :], (B, S, MLA_HEADS, MLA_ROPE))], axis=-1)
        q = rms_norm(q, g_qn[l])
        k = rms_norm(k, g_kn[l])
        q = jnp.concatenate([q[..., :MLA_NOPE], apply_rope(q[..., MLA_NOPE:], cos, sin)], axis=-1)
        k = jnp.concatenate([k[..., :MLA_NOPE], apply_rope(k[..., MLA_NOPE:], cos, sin)], axis=-1)
        o_attn = block_causal_attention(q, k, v)
        o_rnn = jax.nn.gelu(u_gate) * rg_lru(u_x, conv_w[l], conv_b[l], w_rg[l], b_rg[l],
                                             w_ig[l], b_ig[l], lam[l])
        mix = jnp.concatenate([rms_norm(o_attn, g_attn_out[l]), rms_norm(o_rnn, g_rnn_out[l])], axis=-1)
        x = x + mix @ w_out[l]
        x = x + mem_cross_attention(rms_norm(x, g_xq[l]), rms_norm(mem, g_mem[l]),
                                    w_mq[l], w_mk[l], w_mv[l], g_mqn[l], g_mkn[l], w_mo[l])
        x = x + hier_moe(rms_norm(x, g_ffn[l]), w_group[l], b_group[l], w_expert[l], b_expert[l],
                         w_e_gate[l], w_e_up[l], w_e_down[l])
    return x
```

```python
import functools

import jax
import jax.numpy as jnp
from jax import lax
from jax.experimental import pallas as pl
from jax.experimental.pallas import tpu as pltpu

F32 = jnp.float32
BF16 = jnp.bfloat16

EPS = 1e-6
LANES = 128
CHUNK_SHIFT = 6
MLA_HEADS = 8
MLA_NOPE = 64
MLA_ROPE = 32
MLA_QK = MLA_NOPE + MLA_ROPE
MLA_V = 64
Q_LORA = 256
KV_LORA = 128
RNN_WIDTH = 512
RNN_BLOCKS = 8
CONV_WIDTH = 4
LRU_C = 8.0
ROPE_BASE = 10000.0
MEM_HEADS = 4
MEM_HEAD_DIM = 256
N_GROUPS = 4
EXPERTS_PER_GROUP = 8
N_EXPERTS = 32
D_EXPERT = 256
ROUTER_OFF = N_GROUPS
NEG = -1e30
VMEM_LIMIT = 56 * 1024 * 1024


def _rms(x, g):
    return x * lax.rsqrt(jnp.mean(x * x, axis=-1, keepdims=True) + EPS) * g


def _bdot(a, b):
    return jnp.dot(a.astype(BF16), b, preferred_element_type=F32)


def _cparams(sem):
    return pltpu.CompilerParams(dimension_semantics=sem, vmem_limit_bytes=VMEM_LIMIT)


def _const_spec(shape):
    return pl.BlockSpec(shape, lambda *_: (0,) * len(shape))


def _proj_kernel(x_ref, gmix_ref, win_ref, gcq_ref, wuq_ref, gckv_ref, wukv_ref, gqn_ref, gkn_ref,
                 cos_ref, s1_ref, s2_ref, q_ref, k_ref, v_ref, ug_ref, ux_ref):
    x = x_ref[0]
    z = _bdot(_rms(x, gmix_ref[...]), win_ref[...])
    ug_ref[0] = z[:, 512:1024]
    ux_ref[0] = z[:, 1024:1536]
    q = _bdot(_rms(z[:, 0:Q_LORA], gcq_ref[...]), wuq_ref[...])
    kv = _bdot(_rms(z[:, Q_LORA:Q_LORA + KV_LORA], gckv_ref[...]), wukv_ref[...])
    krope = z[:, 384:512]
    cos, s1, s2 = cos_ref[...], s1_ref[...], s2_ref[...]
    gqn, gkn = gqn_ref[...], gkn_ref[...]
    ones_col = (lax.broadcasted_iota(jnp.int32, (1, LANES), 1) == MLA_V).astype(F32)
    scale = MLA_QK ** -0.5

    def norm_rope(t, g):
        t = t * lax.rsqrt(jnp.sum(t * t, axis=-1, keepdims=True) * (1.0 / MLA_QK) + EPS) * g
        return t * cos + pltpu.roll(t, LANES - MLA_ROPE // 2, axis=1) * s1 + pltpu.roll(t, MLA_ROPE // 2, axis=1) * s2

    for h in range(MLA_HEADS):
        sl = slice(h * LANES, (h + 1) * LANES)
        q_ref[0, h] = (norm_rope(q[:, sl], gqn) * scale).astype(BF16)
        k_ref[0, h] = norm_rope(kv[:, sl] + krope, gkn).astype(BF16)
        v_ref[0, h] = (kv[:, MLA_HEADS * LANES + h * LANES:MLA_HEADS * LANES + (h + 1) * LANES] + ones_col).astype(BF16)


def _proj_call(x, gmix, win, gcq, wuq, gckv, wukv, gqn, gkn, cos_t, s1_t, s2_t, tm):
    B, S, D = x.shape
    H = MLA_HEADS
    hs = jax.ShapeDtypeStruct((B, H, S, LANES), BF16)
    us = jax.ShapeDtypeStruct((B, S, RNN_WIDTH), F32)
    hspec = pl.BlockSpec((1, H, tm, LANES), lambda b, i: (b, 0, i, 0))
    uspec = pl.BlockSpec((1, tm, RNN_WIDTH), lambda b, i: (b, i, 0))
    tspec = pl.BlockSpec((tm, LANES), lambda b, i: (i, 0))
    return pl.pallas_call(
        _proj_kernel,
        grid=(B, S // tm),
        in_specs=[pl.BlockSpec((1, tm, D), lambda b, i: (b, i, 0)),
                  _const_spec(gmix.shape), _const_spec(win.shape), _const_spec(gcq.shape),
                  _const_spec(wuq.shape), _const_spec(gckv.shape), _const_spec(wukv.shape),
                  _const_spec(gqn.shape), _const_spec(gkn.shape), tspec, tspec, tspec],
        out_specs=[hspec, hspec, hspec, uspec, uspec],
        out_shape=[hs, hs, hs, us, us],
        compiler_params=_cparams(("parallel", "parallel")),
        name="mla_rglru_in_proj",
    )(x, gmix, win, gcq, wuq, gckv, wukv, gqn, gkn, cos_t, s1_t, s2_t)


def _rglru_kernel(ug_ref, ux_ref, cw_ref, cb_ref, wrg_ref, brg_ref, wig_ref, big_ref, lam_ref, gout_ref,
                  o_ref, ubuf, a_s, b_s, h_s, hc_s, *, tm):
    W = RNN_WIDTH

    @pl.when(pl.program_id(1) == 0)
    def _():
        ubuf[0:8, :] = jnp.zeros((8, W), F32)
        hc_s[...] = jnp.zeros((8, W), F32)

    ubuf[8:tm + 8, :] = ux_ref[0]
    cw = cw_ref[...]
    xc = cb_ref[...] + sum(ubuf[pl.ds(8 - (CONV_WIDTH - 1) + j, tm), :] * cw[j:j + 1, :] for j in range(CONV_WIDTH))
    ubuf[0:8, :] = ubuf[tm:tm + 8, :]
    r = jax.nn.sigmoid(_bdot(xc, wrg_ref[...]) + brg_ref[...])
    i = jax.nn.sigmoid(_bdot(xc, wig_ref[...]) + big_ref[...])
    nl = -lam_ref[...]
    softplus = jnp.maximum(nl, 0.0) + jnp.log(1.0 + jnp.exp(-jnp.abs(nl)))
    a = jnp.exp((-LRU_C * r) * softplus)
    bt = jnp.sqrt(1.0 - a * a) * (i * xc)

    a3 = a.reshape(tm // 8, 8, W)
    b3 = bt.reshape(tm // 8, 8, W)
    row = lax.broadcasted_iota(jnp.int32, (tm // 8, 8, W), 1)
    for d in (1, 2, 4):
        keep = row >= d
        a_sh = jnp.where(keep, pltpu.roll(a3, d, axis=1), 1.0)
        b_sh = jnp.where(keep, pltpu.roll(b3, d, axis=1), 0.0)
        b3 = a3 * b_sh + b3
        a3 = a3 * a_sh
    a_s[...] = a3.reshape(tm, W)
    b_s[...] = b3.reshape(tm, W)

    def body(g, hprev):
        sl = pl.ds(pl.multiple_of(g * 8, 8), 8)
        hr = a_s[sl, :] * hprev + b_s[sl, :]
        h_s[sl, :] = hr
        return jnp.broadcast_to(hr[7:8, :], (8, W))

    hc_s[...] = lax.fori_loop(0, tm // 8, body, hc_s[...])
    o = jax.nn.gelu(ug_ref[0], approximate=True) * h_s[...]
    o_ref[0] = _rms(o, gout_ref[...]).astype(BF16)


def _rglru_call(ug, ux, cw, cb, wrg, brg, wig, big, lam, gout, tm):
    B, S, W = ux.shape
    uspec = pl.BlockSpec((1, tm, W), lambda b, i: (b, i, 0))
    consts = (cw, cb, wrg, brg, wig, big, lam, gout)
    return pl.pallas_call(
        functools.partial(_rglru_kernel, tm=tm),
        grid=(B, S // tm),
        in_specs=[uspec, uspec] + [_const_spec(c.shape) for c in consts],
        out_specs=uspec,
        out_shape=jax.ShapeDtypeStruct((B, S, W), BF16),
        scratch_shapes=[pltpu.VMEM((tm + 8, W), F32), pltpu.VMEM((tm, W), F32), pltpu.VMEM((tm, W), F32),
                        pltpu.VMEM((tm, W), F32), pltpu.VMEM((8, W), F32)],
        compiler_params=_cparams(("parallel", "arbitrary")),
        name="rglru",
    )(ug, ux, *consts)


def _attn_kernel(q_ref, k_ref, v_ref, o_ref, m_s, acc_s, *, tq, tk, hp):
    qi = pl.program_id(2)
    q0 = qi * tq
    n_full = q0 // tk
    outs = []
    for hh in range(hp):
        q = q_ref[0, hh]
        m_s[...] = jnp.full((tq, LANES), NEG, F32)
        acc_s[...] = jnp.zeros((tq, LANES), F32)

        def step(kb, masked):
            ks = pl.multiple_of(kb * tk, tk)
            k = k_ref[0, hh, pl.ds(ks, tk), :]
            v = v_ref[0, hh, pl.ds(ks, tk), :]
            s = lax.dot_general(q, k, (((1,), (1,)), ((), ())), preferred_element_type=F32)
            if masked:
                rows = q0 + lax.broadcasted_iota(jnp.int32, (tq, tk), 0)
                cols = ks + lax.broadcasted_iota(jnp.int32, (tq, tk), 1)
                s = jnp.where((cols >> CHUNK_SHIFT) <= (rows >> CHUNK_SHIFT), s, NEG)
            m_prev = m_s[...]
            m_next = jnp.maximum(m_prev, jnp.max(s, axis=1, keepdims=True))
            alpha = jnp.exp(m_prev - m_next)
            p = jnp.exp(s - pltpu.repeat(m_next, tk // LANES, axis=1))
            acc_s[...] = acc_s[...] * alpha + jnp.dot(p.astype(BF16), v, preferred_element_type=F32)
            m_s[...] = m_next

        def full_body(kb, carry):
            step(kb, False)
            return carry

        lax.fori_loop(0, n_full, full_body, 0)
        for j in range(tq // tk):
            step(n_full + j, True)
        acc = acc_s[...]
        outs.append(acc / acc[:, MLA_V:MLA_V + 1])
    lane = lax.broadcasted_iota(jnp.int32, (tq, LANES), 1)
    for pr in range(hp // 2):
        o_ref[0, :, pr * LANES:(pr + 1) * LANES] = jnp.where(
            lane < MLA_V, outs[2 * pr], pltpu.roll(outs[2 * pr + 1], MLA_V, axis=1)).astype(BF16)


def _attn_call(q, k, v, tq, tk, hp):
    B, H, S, _ = q.shape
    return pl.pallas_call(
        functools.partial(_attn_kernel, tq=tq, tk=tk, hp=hp),
        grid=(B, H // hp, S // tq),
        in_specs=[pl.BlockSpec((1, hp, tq, LANES), lambda b, h, i: (b, h, i, 0)),
                  pl.BlockSpec((1, hp, S, LANES), lambda b, h, i: (b, h, 0, 0)),
                  pl.BlockSpec((1, hp, S, LANES), lambda b, h, i: (b, h, 0, 0))],
        out_specs=pl.BlockSpec((1, tq, hp * MLA_V), lambda b, h, i: (b, i, h)),
        out_shape=jax.ShapeDtypeStruct((B, S, H * MLA_V), BF16),
        scratch_shapes=[pltpu.VMEM((tq, LANES), F32), pltpu.VMEM((tq, LANES), F32)],
        compiler_params=_cparams(("parallel", "parallel", "arbitrary")),
        name="block_causal_attention",
    )(q, k, v)


def _memkv_kernel(mem_ref, gmem_ref, wmk_ref, wmv_ref, gmkn_ref, k_ref, v_ref):
    m = _rms(mem_ref[0], gmem_ref[...]).astype(BF16)
    k = jnp.dot(m, wmk_ref[...], preferred_element_type=F32)
    v = jnp.dot(m, wmv_ref[...], preferred_element_type=F32)
    g = gmkn_ref[...]
    for h in range(MEM_HEADS):
        sl = slice(h * MEM_HEAD_DIM, (h + 1) * MEM_HEAD_DIM)
        k_ref[0, :, sl] = _rms(k[:, sl], g).astype(BF16)
    v_ref[0] = v.astype(BF16)


def _memkv_call(mem, gmem, wmk, wmv, gmkn):
    B, M, D = mem.shape
    spec = pl.BlockSpec((1, M, D), lambda b: (b, 0, 0))
    os_ = jax.ShapeDtypeStruct((B, M, D), BF16)
    return pl.pallas_call(
        _memkv_kernel,
        grid=(B,),
        in_specs=[spec, _const_spec(gmem.shape), _const_spec(wmk.shape), _const_spec(wmv.shape),
                  _const_spec(gmkn.shape)],
        out_specs=[spec, spec],
        out_shape=[os_, os_],
        compiler_params=_cparams(("parallel",)),
        name="mem_kv",
    )(mem, gmem, wmk, wmv, gmkn)


def _mix_mem_kernel(x_ref, oa_ref, orn_ref, gao_ref, wout_ref, gxq_ref, wmq_ref, gmqn_ref, km_ref, vm_ref,
                    wmo_ref, o_ref):
    oa = _rms(oa_ref[0].astype(F32), gao_ref[...]).astype(BF16)
    mix = jnp.concatenate([oa, orn_ref[0]], axis=-1)
    x1 = x_ref[0] + jnp.dot(mix, wout_ref[...], preferred_element_type=F32)
    qm = _bdot(_rms(x1, gxq_ref[...]), wmq_ref[...])
    g = gmqn_ref[...]
    scale = MEM_HEAD_DIM ** -0.5
    heads = []
    for h in range(MEM_HEADS):
        sl = slice(h * MEM_HEAD_DIM, (h + 1) * MEM_HEAD_DIM)
        qh = (_rms(qm[:, sl], g) * scale).astype(BF16)
        s = lax.dot_general(qh, km_ref[0, :, sl], (((1,), (1,)), ((), ())), preferred_element_type=F32)
        p = jnp.exp(s - jnp.max(s, axis=-1, keepdims=True))
        oh = jnp.dot(p.astype(BF16), vm_ref[0, :, sl], preferred_element_type=F32)
        heads.append((oh / jnp.sum(p, axis=-1, keepdims=True)).astype(BF16))
    o_ref[0] = x1 + jnp.dot(jnp.concatenate(heads, axis=-1), wmo_ref[...], preferred_element_type=F32)


def _mix_mem_call(x, oa, orn, gao, wout, gxq, wmq, gmqn, km, vm, wmo, tm):
    B, S, D = x.shape
    M = km.shape[1]
    xspec = pl.BlockSpec((1, tm, D), lambda b, i: (b, i, 0))
    hspec = pl.BlockSpec((1, tm, D // 2), lambda b, i: (b, i, 0))
    mspec = pl.BlockSpec((1, M, D), lambda b, i: (b, 0, 0))
    return pl.pallas_call(
        _mix_mem_kernel,
        grid=(B, S // tm),
        in_specs=[xspec, hspec, hspec, _const_spec(gao.shape), _const_spec(wout.shape), _const_spec(gxq.shape),
                  _const_spec(wmq.shape), _const_spec(gmqn.shape), mspec, mspec, _const_spec(wmo.shape)],
        out_specs=xspec,
        out_shape=jax.ShapeDtypeStruct((B, S, D), F32),
        compiler_params=_cparams(("parallel", "parallel")),
        name="out_proj_mem_attn",
    )(x, oa, orn, gao, wout, gxq, wmq, gmqn, km, vm, wmo)


def _route(logits):
    lane = lax.broadcasted_iota(jnp.int32, logits.shape, 1)
    is_g = lane < N_GROUPS
    gmax = jnp.max(jnp.where(is_g, logits, NEG), axis=-1, keepdims=True)
    gsum = jnp.sum(jnp.where(is_g, jnp.exp(logits - gmax), 0.0), axis=-1, keepdims=True)
    p_g = 1.0 / gsum
    g_idx = jnp.min(jnp.where(is_g & (logits == gmax), lane, LANES), axis=-1, keepdims=True)
    e_lo = ROUTER_OFF + g_idx * EXPERTS_PER_GROUP
    sel = (lane >= e_lo) & (lane < e_lo + EXPERTS_PER_GROUP)
    emax = jnp.max(jnp.where(sel, logits, NEG), axis=-1, keepdims=True)
    ee = jnp.where(sel, jnp.exp(logits - emax), 0.0)
    probs = ee / jnp.sum(ee, axis=-1, keepdims=True)
    v1 = jnp.max(jnp.where(sel, probs, -1.0), axis=-1, keepdims=True)
    i1 = jnp.min(jnp.where(sel & (probs == v1), lane, LANES), axis=-1, keepdims=True)
    sel2 = sel & (lane != i1)
    v2 = jnp.max(jnp.where(sel2, probs, -1.0), axis=-1, keepdims=True)
    i2 = jnp.min(jnp.where(sel2 & (probs == v2), lane, LANES), axis=-1, keepdims=True)
    tot = v1 + v2
    return jnp.where(lane == i1, p_g * (v1 / tot), 0.0) + jnp.where(lane == i2, p_g * (v2 / tot), 0.0)


def _moe_kernel(x_ref, gffn_ref, wr_ref, br_ref, wgu_ref, wd_ref, o_ref, t_s, gate_s):
    e = pl.program_id(1)

    @pl.when(e == 0)
    def _():
        x = x_ref[...]
        t = _rms(x, gffn_ref[...])
        t_s[...] = t.astype(BF16)
        logits = jnp.dot(t, wr_ref[...], preferred_element_type=F32, precision=lax.Precision.HIGHEST)
        gate_s[...] = _route(logits + br_ref[...])
        o_ref[...] = x

    lane = lax.broadcasted_iota(jnp.int32, gate_s.shape, 1)
    gate = jnp.sum(jnp.where(lane == ROUTER_OFF + e, gate_s[...], 0.0), axis=-1, keepdims=True)
    h = jnp.dot(t_s[...], wgu_ref[0], preferred_element_type=F32)
    he = jax.nn.silu(h[:, :D_EXPERT]) * h[:, D_EXPERT:]
    o_ref[...] += gate * _bdot(he, wd_ref[0])


def _moe_call(x2, gffn, wr, br, wgu, wd, tm):
    N, D = x2.shape
    xspec = pl.BlockSpec((tm, D), lambda i, e: (i, 0))
    return pl.pallas_call(
        _moe_kernel,
        grid=(N // tm, N_EXPERTS),
        in_specs=[xspec, _const_spec(gffn.shape), _const_spec(wr.shape), _const_spec(br.shape),
                  pl.BlockSpec((1, D, 2 * D_EXPERT), lambda i, e: (e, 0, 0)),
                  pl.BlockSpec((1, D_EXPERT, D), lambda i, e: (e, 0, 0))],
        out_specs=xspec,
        out_shape=jax.ShapeDtypeStruct((N, D), F32),
        scratch_shapes=[pltpu.VMEM((tm, D), BF16), pltpu.VMEM((tm, LANES), F32)],
        compiler_params=_cparams(("parallel", "arbitrary")),
        name="hier_moe_dense",
    )(x2, gffn, wr, br, wgu, wd)


def _pad_lanes(w, width):
    return jnp.pad(w, [(0, 0)] * (w.ndim - 1) + [(0, width - w.shape[-1])])


def _rope_tables(S):
    pos = jnp.arange(S, dtype=F32)
    inv_freq = ROPE_BASE ** (-jnp.arange(0, MLA_ROPE, 2, dtype=F32) / MLA_ROPE)
    ang = pos[:, None] * inv_freq[None, :]
    cos, sin = jnp.cos(ang), jnp.sin(ang)
    half = MLA_ROPE // 2
    z = lambda n: jnp.zeros((S, n), F32)
    cos_t = jnp.concatenate([jnp.ones((S, MLA_NOPE), F32), cos, cos, z(LANES - MLA_QK)], axis=1)
    s1_t = jnp.concatenate([z(MLA_NOPE), -sin, z(LANES - MLA_NOPE - half)], axis=1)
    s2_t = jnp.concatenate([z(MLA_NOPE + half), sin, z(LANES - MLA_QK)], axis=1)
    return cos_t, s1_t, s2_t


def _block_diag(w):
    n, c, d = w.shape
    eye = jnp.eye(n, dtype=w.dtype)
    return (eye[:, None, :, None] * w[:, :, None, :]).reshape(n * c, n * d)


def kernel(x, mem, g_mix, w_in, g_cq, w_uq, g_ckv, w_ukv, g_qn, g_kn, conv_w, conv_b, w_rg, b_rg, w_ig, b_ig,
           lam, g_attn_out, g_rnn_out, w_out, g_xq, g_mem, w_mq, w_mk, w_mv, g_mqn, g_mkn, w_mo, g_ffn,
           w_group, b_group, w_expert, b_expert, w_e_gate, w_e_up, w_e_down):
    B, S, D = x.shape
    H = MLA_HEADS
    tm = min(512, S)
    tq = tk = min(512, S)
    row = lambda a: a.reshape(1, -1)
    cos_t, s1_t, s2_t = _rope_tables(S)
    for l in range(g_mix.shape[0]):
        wi = w_in[l]
        c0, c1, c2 = Q_LORA + KV_LORA, Q_LORA + KV_LORA + MLA_ROPE, Q_LORA + KV_LORA + MLA_ROPE + RNN_WIDTH
        zc = lambda n: jnp.zeros((D, n), wi.dtype)
        win = jnp.concatenate([wi[:, :c0], zc(MLA_NOPE), wi[:, c0:c1], zc(LANES - MLA_QK), wi[:, c1:c2], wi[:, c2:]],
                              axis=1).astype(BF16)
        wuq = _pad_lanes(w_uq[l].reshape(Q_LORA, H, MLA_QK), LANES).reshape(Q_LORA, H * LANES).astype(BF16)
        wkv = w_ukv[l].reshape(KV_LORA, H, MLA_NOPE + MLA_V)
        wukv = jnp.concatenate([_pad_lanes(wkv[:, :, :MLA_NOPE], LANES).reshape(KV_LORA, H * LANES),
                                _pad_lanes(wkv[:, :, MLA_NOPE:], LANES).reshape(KV_LORA, H * LANES)],
                               axis=1).astype(BF16)
        gqn = _pad_lanes(row(g_qn[l]), LANES)
        gkn = _pad_lanes(row(g_kn[l]), LANES)
        wrg = _block_diag(w_rg[l]).astype(BF16)
        wig = _block_diag(w_ig[l]).astype(BF16)
        wr = _pad_lanes(jnp.concatenate([w_group[l], w_expert[l]], axis=1), LANES)
        br = _pad_lanes(row(jnp.concatenate([b_group[l], b_expert[l]])), LANES)
        wgu = jnp.concatenate([w_e_gate[l], w_e_up[l]], axis=-1).astype(BF16)
        wd = w_e_down[l].astype(BF16)

        q, k, v, ug, ux = _proj_call(x, row(g_mix[l]), win, row(g_cq[l]), wuq, row(g_ckv[l]), wukv, gqn, gkn,
                                     cos_t, s1_t, s2_t, tm)
        o_rnn = _rglru_call(ug, ux, conv_w[l], row(conv_b[l]), wrg, row(b_rg[l]), wig, row(b_ig[l]), row(lam[l]),
                            row(g_rnn_out[l]), tm)
        o_attn = _attn_call(q, k, v, tq, tk, 2)
        km, vm = _memkv_call(mem, row(g_mem[l]), w_mk[l].astype(BF16), w_mv[l].astype(BF16), row(g_mkn[l]))
        x2 = _mix_mem_call(x, o_attn, o_rnn, row(g_attn_out[l]), w_out[l].astype(BF16), row(g_xq[l]),
                           w_mq[l].astype(BF16), row(g_mqn[l]), km, vm, w_mo[l].astype(BF16), tm)
        x = _moe_call(x2.reshape(B * S, D), row(g_ffn[l]), wr, br, wgu, wd, min(1024, B * S)).reshape(B, S, D)
    return x
```

```python
import functools

import jax
import jax.numpy as jnp
from jax import lax
from jax.experimental import pallas as pl
from jax.experimental.pallas import tpu as pltpu

F32 = jnp.float32
BF16 = jnp.bfloat16

EPS = 1e-6
LANES = 128
CHUNK_SHIFT = 6
MLA_HEADS = 8
MLA_NOPE = 64
MLA_ROPE = 32
MLA_QK = MLA_NOPE + MLA_ROPE
MLA_V = 64
Q_LORA = 256
KV_LORA = 128
RNN_WIDTH = 512
RNN_BLOCKS = 8
CONV_WIDTH = 4
LRU_C = 8.0
ROPE_BASE = 10000.0
MEM_HEADS = 4
MEM_HEAD_DIM = 256
N_GROUPS = 4
EXPERTS_PER_GROUP = 8
N_EXPERTS = 32
D_EXPERT = 256
ROUTER_OFF = N_GROUPS
NEG = -1e30
LOG2E = 1.4426950408889634
VMEM_LIMIT = 56 * 1024 * 1024


def _rms(x, g):
    return x * lax.rsqrt(jnp.mean(x * x, axis=-1, keepdims=True) + EPS) * g


def _bdot(a, b):
    return jnp.dot(a.astype(BF16), b, preferred_element_type=F32)


def _cparams(sem):
    return pltpu.CompilerParams(dimension_semantics=sem, vmem_limit_bytes=VMEM_LIMIT)


def _const_spec(shape):
    return pl.BlockSpec(shape, lambda *_: (0,) * len(shape))


def _proj_kernel(x_ref, gmix_ref, win_ref, gcq_ref, wuq_ref, gckv_ref, wukv_ref, gqn_ref, gkn_ref,
                 cos_ref, s1_ref, s2_ref, q_ref, k_ref, v_ref, ug_ref, ux_ref):
    x = x_ref[0]
    z = _bdot(_rms(x, gmix_ref[...]), win_ref[...])
    ug_ref[0] = z[:, 512:1024]
    ux_ref[0] = z[:, 1024:1536]
    q = _bdot(_rms(z[:, 0:Q_LORA], gcq_ref[...]), wuq_ref[...])
    kv = _bdot(_rms(z[:, Q_LORA:Q_LORA + KV_LORA], gckv_ref[...]), wukv_ref[...])
    krope = z[:, 384:512]
    cos, s1, s2 = cos_ref[...], s1_ref[...], s2_ref[...]
    gqn, gkn = gqn_ref[...], gkn_ref[...]
    ones_col = (lax.broadcasted_iota(jnp.int32, (1, LANES), 1) == MLA_V).astype(F32)
    scale = MLA_QK ** -0.5 * LOG2E

    def norm_rope(t, g):
        t = t * lax.rsqrt(jnp.sum(t * t, axis=-1, keepdims=True) * (1.0 / MLA_QK) + EPS) * g
        return t * cos + pltpu.roll(t, LANES - MLA_ROPE // 2, axis=1) * s1 + pltpu.roll(t, MLA_ROPE // 2, axis=1) * s2

    for h in range(MLA_HEADS):
        sl = slice(h * LANES, (h + 1) * LANES)
        q_ref[0, h] = (norm_rope(q[:, sl], gqn) * scale).astype(BF16)
        k_ref[0, h] = norm_rope(kv[:, sl] + krope, gkn).astype(BF16)
        v_ref[0, h] = (kv[:, MLA_HEADS * LANES + h * LANES:MLA_HEADS * LANES + (h + 1) * LANES] + ones_col).astype(BF16)


def _proj_call(x, gmix, win, gcq, wuq, gckv, wukv, gqn, gkn, cos_t, s1_t, s2_t, tm):
    B, S, D = x.shape
    H = MLA_HEADS
    hs = jax.ShapeDtypeStruct((B, H, S, LANES), BF16)
    us = jax.ShapeDtypeStruct((B, S, RNN_WIDTH), F32)
    hspec = pl.BlockSpec((1, H, tm, LANES), lambda b, i: (b, 0, i, 0))
    uspec = pl.BlockSpec((1, tm, RNN_WIDTH), lambda b, i: (b, i, 0))
    tspec = pl.BlockSpec((tm, LANES), lambda b, i: (i, 0))
    return pl.pallas_call(
        _proj_kernel,
        grid=(B, S // tm),
        in_specs=[pl.BlockSpec((1, tm, D), lambda b, i: (b, i, 0)),
                  _const_spec(gmix.shape), _const_spec(win.shape), _const_spec(gcq.shape),
                  _const_spec(wuq.shape), _const_spec(gckv.shape), _const_spec(wukv.shape),
                  _const_spec(gqn.shape), _const_spec(gkn.shape), tspec, tspec, tspec],
        out_specs=[hspec, hspec, hspec, uspec, uspec],
        out_shape=[hs, hs, hs, us, us],
        compiler_params=_cparams(("parallel", "parallel")),
        name="mla_rglru_in_proj",
    )(x, gmix, win, gcq, wuq, gckv, wukv, gqn, gkn, cos_t, s1_t, s2_t)


def _rglru_kernel(ug_ref, ux_ref, cw_ref, cb_ref, wrg_ref, brg_ref, wig_ref, big_ref, lam_ref, gout_ref,
                  o_ref, ubuf, a_s, b_s, h_s, hc_s, *, tm):
    W = RNN_WIDTH

    @pl.when(pl.program_id(1) == 0)
    def _():
        ubuf[0:8, :] = jnp.zeros((8, W), F32)
        hc_s[...] = jnp.zeros((8, W), F32)

    ubuf[8:tm + 8, :] = ux_ref[0]
    cw = cw_ref[...]
    xc = cb_ref[...] + sum(ubuf[pl.ds(8 - (CONV_WIDTH - 1) + j, tm), :] * cw[j:j + 1, :] for j in range(CONV_WIDTH))
    ubuf[0:8, :] = ubuf[tm:tm + 8, :]
    r = jax.nn.sigmoid(_bdot(xc, wrg_ref[...]) + brg_ref[...])
    i = jax.nn.sigmoid(_bdot(xc, wig_ref[...]) + big_ref[...])
    nl = -lam_ref[...]
    softplus = jnp.maximum(nl, 0.0) + jnp.log(1.0 + jnp.exp(-jnp.abs(nl)))
    a = jnp.exp((-LRU_C * r) * softplus)
    bt = jnp.sqrt(1.0 - a * a) * (i * xc)

    a3 = a.reshape(tm // 8, 8, W)
    b3 = bt.reshape(tm // 8, 8, W)
    row = lax.broadcasted_iota(jnp.int32, (tm // 8, 8, W), 1)
    for d in (1, 2, 4):
        keep = row >= d
        a_sh = jnp.where(keep, pltpu.roll(a3, d, axis=1), 1.0)
        b_sh = jnp.where(keep, pltpu.roll(b3, d, axis=1), 0.0)
        b3 = a3 * b_sh + b3
        a3 = a3 * a_sh
    a_s[...] = a3.reshape(tm, W)
    b_s[...] = b3.reshape(tm, W)

    def body(g, hprev):
        sl = pl.ds(pl.multiple_of(g * 8, 8), 8)
        hr = a_s[sl, :] * hprev + b_s[sl, :]
        h_s[sl, :] = hr
        return jnp.broadcast_to(hr[7:8, :], (8, W))

    hc_s[...] = lax.fori_loop(0, tm // 8, body, hc_s[...])
    o = jax.nn.gelu(ug_ref[0], approximate=True) * h_s[...]
    o_ref[0] = _rms(o, gout_ref[...]).astype(BF16)


def _rglru_call(ug, ux, cw, cb, wrg, brg, wig, big, lam, gout, tm):
    B, S, W = ux.shape
    uspec = pl.BlockSpec((1, tm, W), lambda b, i: (b, i, 0))
    consts = (cw, cb, wrg, brg, wig, big, lam, gout)
    return pl.pallas_call(
        functools.partial(_rglru_kernel, tm=tm),
        grid=(B, S // tm),
        in_specs=[uspec, uspec] + [_const_spec(c.shape) for c in consts],
        out_specs=uspec,
        out_shape=jax.ShapeDtypeStruct((B, S, W), BF16),
        scratch_shapes=[pltpu.VMEM((tm + 8, W), F32), pltpu.VMEM((tm, W), F32), pltpu.VMEM((tm, W), F32),
                        pltpu.VMEM((tm, W), F32), pltpu.VMEM((8, W), F32)],
        compiler_params=_cparams(("parallel", "arbitrary")),
        name="rglru",
    )(ug, ux, *consts)


def _attn_kernel(q_ref, k_ref, v_ref, bias_ref, o_ref, m_s, acc_s, *, tq, tk, hp):
    qi = pl.program_id(2)
    nsub = tq // tk
    n_full = qi * nsub
    m_s[...] = jnp.full(m_s.shape, NEG, F32)
    acc_s[...] = jnp.zeros(acc_s.shape, F32)

    def block(hh, kb, r0, nrows, masked):
        ks = pl.multiple_of(kb * tk, tk)
        rows = slice(r0, r0 + nrows)
        s = lax.dot_general(q_ref[0, hh, rows, :], k_ref[0, hh, pl.ds(ks, tk), :],
                            (((1,), (1,)), ((), ())), preferred_element_type=F32)
        if masked:
            s = s + bias_ref[...]
        m_prev = m_s[hh, rows, :]
        m_next = jnp.maximum(m_prev, jnp.max(s, axis=1, keepdims=True))
        alpha = jnp.exp2(m_prev - m_next)
        p = jnp.exp2(s - pltpu.repeat(m_next, tk // LANES, axis=1))
        pv = jnp.dot(p.astype(BF16), v_ref[0, hh, pl.ds(ks, tk), :], preferred_element_type=F32)
        acc_s[hh, rows, :] = acc_s[hh, rows, :] * alpha + pv
        m_s[hh, rows, :] = m_next

    def full_body(kb, carry):
        for hh in range(hp):
            block(hh, kb, 0, tq, False)
        return carry

    lax.fori_loop(0, n_full, full_body, 0)
    for j in range(nsub):
        for hh in range(hp):
            if j + 1 < nsub:
                block(hh, n_full + j, (j + 1) * tk, tq - (j + 1) * tk, False)
            block(hh, n_full + j, j * tk, tk, True)
    lane = lax.broadcasted_iota(jnp.int32, (tq, LANES), 1)
    for pr in range(hp // 2):
        a0, a1 = acc_s[2 * pr], acc_s[2 * pr + 1]
        o0 = a0 / a0[:, MLA_V:MLA_V + 1]
        o1 = a1 / a1[:, MLA_V:MLA_V + 1]
        o_ref[0, :, pr * LANES:(pr + 1) * LANES] = jnp.where(
            lane < MLA_V, o0, pltpu.roll(o1, MLA_V, axis=1)).astype(BF16)


def _attn_call(q, k, v, tq, tk, hp):
    B, H, S, _ = q.shape
    idx = jnp.arange(tk, dtype=jnp.int32) >> CHUNK_SHIFT
    bias = jnp.where(idx[None, :] <= idx[:, None], 0.0, NEG).astype(F32)
    return pl.pallas_call(
        functools.partial(_attn_kernel, tq=tq, tk=tk, hp=hp),
        grid=(B, H // hp, S // tq),
        in_specs=[pl.BlockSpec((1, hp, tq, LANES), lambda b, h, i: (b, h, i, 0)),
                  pl.BlockSpec((1, hp, S, LANES), lambda b, h, i: (b, h, 0, 0)),
                  pl.BlockSpec((1, hp, S, LANES), lambda b, h, i: (b, h, 0, 0)),
                  _const_spec(bias.shape)],
        out_specs=pl.BlockSpec((1, tq, hp * MLA_V), lambda b, h, i: (b, i, h)),
        out_shape=jax.ShapeDtypeStruct((B, S, H * MLA_V), BF16),
        scratch_shapes=[pltpu.VMEM((hp, tq, LANES), F32), pltpu.VMEM((hp, tq, LANES), F32)],
        compiler_params=_cparams(("parallel", "parallel", "arbitrary")),
        name="block_causal_attention",
    )(q, k, v, bias)


def _memkv_kernel(mem_ref, gmem_ref, wmk_ref, wmv_ref, gmkn_ref, k_ref, v_ref):
    m = _rms(mem_ref[0], gmem_ref[...]).astype(BF16)
    k = jnp.dot(m, wmk_ref[...], preferred_element_type=F32)
    v = jnp.dot(m, wmv_ref[...], preferred_element_type=F32)
    g = gmkn_ref[...]
    for h in range(MEM_HEADS):
        sl = slice(h * MEM_HEAD_DIM, (h + 1) * MEM_HEAD_DIM)
        k_ref[0, :, sl] = _rms(k[:, sl], g).astype(BF16)
    v_ref[0] = v.astype(BF16)


def _memkv_call(mem, gmem, wmk, wmv, gmkn):
    B, M, D = mem.shape
    spec = pl.BlockSpec((1, M, D), lambda b: (b, 0, 0))
    os_ = jax.ShapeDtypeStruct((B, M, D), BF16)
    return pl.pallas_call(
        _memkv_kernel,
        grid=(B,),
        in_specs=[spec, _const_spec(gmem.shape), _const_spec(wmk.shape), _const_spec(wmv.shape),
                  _const_spec(gmkn.shape)],
        out_specs=[spec, spec],
        out_shape=[os_, os_],
        compiler_params=_cparams(("parallel",)),
        name="mem_kv",
    )(mem, gmem, wmk, wmv, gmkn)


def _mix_mem_kernel(x_ref, oa_ref, orn_ref, gao_ref, wout_ref, gxq_ref, wmq_ref, gmqn_ref, km_ref, vm_ref,
                    wmo_ref, o_ref):
    oa = _rms(oa_ref[0].astype(F32), gao_ref[...]).astype(BF16)
    mix = jnp.concatenate([oa, orn_ref[0]], axis=-1)
    x1 = x_ref[0] + jnp.dot(mix, wout_ref[...], preferred_element_type=F32)
    qm = _bdot(_rms(x1, gxq_ref[...]), wmq_ref[...])
    g = gmqn_ref[...]
    scale = MEM_HEAD_DIM ** -0.5
    heads = []
    for h in range(MEM_HEADS):
        sl = slice(h * MEM_HEAD_DIM, (h + 1) * MEM_HEAD_DIM)
        qh = (_rms(qm[:, sl], g) * scale).astype(BF16)
        s = lax.dot_general(qh, km_ref[0, :, sl], (((1,), (1,)), ((), ())), preferred_element_type=F32)
        p = jnp.exp(s - jnp.max(s, axis=-1, keepdims=True))
        oh = jnp.dot(p.astype(BF16), vm_ref[0, :, sl], preferred_element_type=F32)
        heads.append((oh / jnp.sum(p, axis=-1, keepdims=True)).astype(BF16))
    o_ref[0] = x1 + jnp.dot(jnp.concatenate(heads, axis=-1), wmo_ref[...], preferred_element_type=F32)


def _mix_mem_call(x, oa, orn, gao, wout, gxq, wmq, gmqn, km, vm, wmo, tm):
    B, S, D = x.shape
    M = km.shape[1]
    xspec = pl.BlockSpec((1, tm, D), lambda b, i: (b, i, 0))
    hspec = pl.BlockSpec((1, tm, D // 2), lambda b, i: (b, i, 0))
    mspec = pl.BlockSpec((1, M, D), lambda b, i: (b, 0, 0))
    return pl.pallas_call(
        _mix_mem_kernel,
        grid=(B, S // tm),
        in_specs=[xspec, hspec, hspec, _const_spec(gao.shape), _const_spec(wout.shape), _const_spec(gxq.shape),
                  _const_spec(wmq.shape), _const_spec(gmqn.shape), mspec, mspec, _const_spec(wmo.shape)],
        out_specs=xspec,
        out_shape=jax.ShapeDtypeStruct((B, S, D), F32),
        compiler_params=_cparams(("parallel", "parallel")),
        name="out_proj_mem_attn",
    )(x, oa, orn, gao, wout, gxq, wmq, gmqn, km, vm, wmo)


def _route(logits):
    lane = lax.broadcasted_iota(jnp.int32, logits.shape, 1)
    is_g = lane < N_GROUPS
    gmax = jnp.max(jnp.where(is_g, logits, NEG), axis=-1, keepdims=True)
    gsum = jnp.sum(jnp.where(is_g, jnp.exp(logits - gmax), 0.0), axis=-1, keepdims=True)
    p_g = 1.0 / gsum
    g_idx = jnp.min(jnp.where(is_g & (logits == gmax), lane, LANES), axis=-1, keepdims=True)
    e_lo = ROUTER_OFF + g_idx * EXPERTS_PER_GROUP
    sel = (lane >= e_lo) & (lane < e_lo + EXPERTS_PER_GROUP)
    emax = jnp.max(jnp.where(sel, logits, NEG), axis=-1, keepdims=True)
    ee = jnp.where(sel, jnp.exp(logits - emax), 0.0)
    probs = ee / jnp.sum(ee, axis=-1, keepdims=True)
    v1 = jnp.max(jnp.where(sel, probs, -1.0), axis=-1, keepdims=True)
    i1 = jnp.min(jnp.where(sel & (probs == v1), lane, LANES), axis=-1, keepdims=True)
    sel2 = sel & (lane != i1)
    v2 = jnp.max(jnp.where(sel2, probs, -1.0), axis=-1, keepdims=True)
    i2 = jnp.min(jnp.where(sel2 & (probs == v2), lane, LANES), axis=-1, keepdims=True)
    tot = v1 + v2
    return jnp.where(lane == i1, p_g * (v1 / tot), 0.0) + jnp.where(lane == i2, p_g * (v2 / tot), 0.0)


def _moe_kernel(x_ref, gffn_ref, wr_ref, br_ref, wgu_ref, wd_ref, o_ref, t_s, gate_s):
    e = pl.program_id(1)

    @pl.when(e == 0)
    def _():
        x = x_ref[...]
        t = _rms(x, gffn_ref[...])
        t_s[...] = t.astype(BF16)
        logits = jnp.dot(t, wr_ref[...], preferred_element_type=F32, precision=lax.Precision.HIGHEST)
        gate_s[...] = _route(logits + br_ref[...])
        o_ref[...] = x

    lane = lax.broadcasted_iota(jnp.int32, gate_s.shape, 1)
    gate = jnp.sum(jnp.where(lane == ROUTER_OFF + e, gate_s[...], 0.0), axis=-1, keepdims=True)
    h = jnp.dot(t_s[...], wgu_ref[0], preferred_element_type=F32)
    he = jax.nn.silu(h[:, :D_EXPERT]) * h[:, D_EXPERT:]
    o_ref[...] += gate * _bdot(he, wd_ref[0])


def _moe_call(x2, gffn, wr, br, wgu, wd, tm):
    N, D = x2.shape
    xspec = pl.BlockSpec((tm, D), lambda i, e: (i, 0))
    return pl.pallas_call(
        _moe_kernel,
        grid=(N // tm, N_EXPERTS),
        in_specs=[xspec, _const_spec(gffn.shape), _const_spec(wr.shape), _const_spec(br.shape),
                  pl.BlockSpec((1, D, 2 * D_EXPERT), lambda i, e: (e, 0, 0)),
                  pl.BlockSpec((1, D_EXPERT, D), lambda i, e: (e, 0, 0))],
        out_specs=xspec,
        out_shape=jax.ShapeDtypeStruct((N, D), F32),
        scratch_shapes=[pltpu.VMEM((tm, D), BF16), pltpu.VMEM((tm, LANES), F32)],
        compiler_params=_cparams(("parallel", "arbitrary")),
        name="hier_moe_dense",
    )(x2, gffn, wr, br, wgu, wd)


def _pad_lanes(w, width):
    return jnp.pad(w, [(0, 0)] * (w.ndim - 1) + [(0, width - w.shape[-1])])


def _rope_tables(S):
    pos = jnp.arange(S, dtype=F32)
    inv_freq = ROPE_BASE ** (-jnp.arange(0, MLA_ROPE, 2, dtype=F32) / MLA_ROPE)
    ang = pos[:, None] * inv_freq[None, :]
    cos, sin = jnp.cos(ang), jnp.sin(ang)
    half = MLA_ROPE // 2
    z = lambda n: jnp.zeros((S, n), F32)
    cos_t = jnp.concatenate([jnp.ones((S, MLA_NOPE), F32), cos, cos, z(LANES - MLA_QK)], axis=1)
    s1_t = jnp.concatenate([z(MLA_NOPE), -sin, z(LANES - MLA_NOPE - half)], axis=1)
    s2_t = jnp.concatenate([z(MLA_NOPE + half), sin, z(LANES - MLA_QK)], axis=1)
    return cos_t, s1_t, s2_t


def _block_diag(w):
    n, c, d = w.shape
    eye = jnp.eye(n, dtype=w.dtype)
    return (eye[:, None, :, None] * w[:, :, None, :]).reshape(n * c, n * d)


def kernel(x, mem, g_mix, w_in, g_cq, w_uq, g_ckv, w_ukv, g_qn, g_kn, conv_w, conv_b, w_rg, b_rg, w_ig, b_ig,
           lam, g_attn_out, g_rnn_out, w_out, g_xq, g_mem, w_mq, w_mk, w_mv, g_mqn, g_mkn, w_mo, g_ffn,
           w_group, b_group, w_expert, b_expert, w_e_gate, w_e_up, w_e_down):
    B, S, D = x.shape
    H = MLA_HEADS
    tm = min(512, S)
    tk = min(512, S)
    tq = min(1024, S)
    row = lambda a: a.reshape(1, -1)
    cos_t, s1_t, s2_t = _rope_tables(S)
    for l in range(g_mix.shape[0]):
        wi = w_in[l]
        c0, c1, c2 = Q_LORA + KV_LORA, Q_LORA + KV_LORA + MLA_ROPE, Q_LORA + KV_LORA + MLA_ROPE + RNN_WIDTH
        zc = lambda n: jnp.zeros((D, n), wi.dtype)
        win = jnp.concatenate([wi[:, :c0], zc(MLA_NOPE), wi[:, c0:c1], zc(LANES - MLA_QK), wi[:, c1:c2], wi[:, c2:]],
                              axis=1).astype(BF16)
        wuq = _pad_lanes(w_uq[l].reshape(Q_LORA, H, MLA_QK), LANES).reshape(Q_LORA, H * LANES).astype(BF16)
        wkv = w_ukv[l].reshape(KV_LORA, H, MLA_NOPE + MLA_V)
        wukv = jnp.concatenate([_pad_lanes(wkv[:, :, :MLA_NOPE], LANES).reshape(KV_LORA, H * LANES),
                                _pad_lanes(wkv[:, :, MLA_NOPE:], LANES).reshape(KV_LORA, H * LANES)],
                               axis=1).astype(BF16)
        gqn = _pad_lanes(row(g_qn[l]), LANES)
        gkn = _pad_lanes(row(g_kn[l]), LANES)
        wrg = _block_diag(w_rg[l]).astype(BF16)
        wig = _block_diag(w_ig[l]).astype(BF16)
        wr = _pad_lanes(jnp.concatenate([w_group[l], w_expert[l]], axis=1), LANES)
        br = _pad_lanes(row(jnp.concatenate([b_group[l], b_expert[l]])), LANES)
        wgu = jnp.concatenate([w_e_gate[l], w_e_up[l]], axis=-1).astype(BF16)
        wd = w_e_down[l].astype(BF16)

        q, k, v, ug, ux = _proj_call(x, row(g_mix[l]), win, row(g_cq[l]), wuq, row(g_ckv[l]), wukv, gqn, gkn,
                                     cos_t, s1_t, s2_t, tm)
        o_rnn = _rglru_call(ug, ux, conv_w[l], row(conv_b[l]), wrg, row(b_rg[l]), wig, row(b_ig[l]), row(lam[l]),
                            row(g_rnn_out[l]), tm)
        o_attn = _attn_call(q, k, v, tq, tk, 2)
        km, vm = _memkv_call(mem, row(g_mem[l]), w_mk[l].astype(BF16), w_mv[l].astype(BF16), row(g_mkn[l]))
        x2 = _mix_mem_call(x, o_attn, o_rnn, row(g_attn_out[l]), w_out[l].astype(BF16), row(g_xq[l]),
                           w_mq[l].astype(BF16), row(g_mqn[l]), km, vm, w_mo[l].astype(BF16), tm)
        x = _moe_call(x2.reshape(B * S, D), row(g_ffn[l]), wr, br, wgu, wd, min(1024, B * S)).reshape(B, S, D)
    return x
```

```python
import functools

import jax
import jax.numpy as jnp
from jax import lax
from jax.experimental import pallas as pl
from jax.experimental.pallas import tpu as pltpu

F32 = jnp.float32
BF16 = jnp.bfloat16

EPS = 1e-6
LANES = 128
CHUNK_SHIFT = 6
MLA_HEADS = 8
MLA_NOPE = 64
MLA_ROPE = 32
MLA_QK = MLA_NOPE + MLA_ROPE
MLA_V = 64
Q_LORA = 256
KV_LORA = 128
RNN_WIDTH = 512
RNN_BLOCKS = 8
CONV_WIDTH = 4
LRU_C = 8.0
ROPE_BASE = 10000.0
MEM_HEADS = 4
MEM_HEAD_DIM = 256
N_GROUPS = 4
EXPERTS_PER_GROUP = 8
N_EXPERTS = 32
D_EXPERT = 256
ROUTER_OFF = 8
MOE_CHUNK = 320
NEG = -1e30
LOG2E = 1.4426950408889634
VMEM_LIMIT = 56 * 1024 * 1024


def _rms(x, g):
    return x * lax.rsqrt(jnp.mean(x * x, axis=-1, keepdims=True) + EPS) * g


def _bdot(a, b):
    return jnp.dot(a.astype(BF16), b, preferred_element_type=F32)


def _cparams(sem):
    return pltpu.CompilerParams(dimension_semantics=sem, vmem_limit_bytes=VMEM_LIMIT)


def _const_spec(shape):
    return pl.BlockSpec(shape, lambda *_: (0,) * len(shape))


def _proj_kernel(x_ref, gmix_ref, win_ref, gcq_ref, wuq_ref, gckv_ref, wukv_ref, gqn_ref, gkn_ref,
                 cos_ref, s1_ref, s2_ref, q_ref, k_ref, v_ref, ug_ref, ux_ref):
    x = x_ref[0]
    z = _bdot(_rms(x, gmix_ref[...]), win_ref[...])
    ug_ref[0] = z[:, 512:1024]
    ux_ref[0] = z[:, 1024:1536]
    q = _bdot(_rms(z[:, 0:Q_LORA], gcq_ref[...]), wuq_ref[...])
    kv = _bdot(_rms(z[:, Q_LORA:Q_LORA + KV_LORA], gckv_ref[...]), wukv_ref[...])
    krope = z[:, 384:512]
    cos, s1, s2 = cos_ref[...], s1_ref[...], s2_ref[...]
    gqn, gkn = gqn_ref[...], gkn_ref[...]
    ones_col = (lax.broadcasted_iota(jnp.int32, (1, LANES), 1) == MLA_V).astype(F32)
    scale = MLA_QK ** -0.5 * LOG2E

    def norm_rope(t, g):
        t = t * lax.rsqrt(jnp.sum(t * t, axis=-1, keepdims=True) * (1.0 / MLA_QK) + EPS) * g
        return t * cos + pltpu.roll(t, LANES - MLA_ROPE // 2, axis=1) * s1 + pltpu.roll(t, MLA_ROPE // 2, axis=1) * s2

    for h in range(MLA_HEADS):
        sl = slice(h * LANES, (h + 1) * LANES)
        q_ref[0, h] = (norm_rope(q[:, sl], gqn) * scale).astype(BF16)
        k_ref[0, h] = norm_rope(kv[:, sl] + krope, gkn).astype(BF16)
        v_ref[0, h] = (kv[:, MLA_HEADS * LANES + h * LANES:MLA_HEADS * LANES + (h + 1) * LANES] + ones_col).astype(BF16)


def _proj_call(x, gmix, win, gcq, wuq, gckv, wukv, gqn, gkn, cos_t, s1_t, s2_t, tm):
    B, S, D = x.shape
    H = MLA_HEADS
    hs = jax.ShapeDtypeStruct((B, H, S, LANES), BF16)
    us = jax.ShapeDtypeStruct((B, S, RNN_WIDTH), F32)
    hspec = pl.BlockSpec((1, H, tm, LANES), lambda b, i: (b, 0, i, 0))
    uspec = pl.BlockSpec((1, tm, RNN_WIDTH), lambda b, i: (b, i, 0))
    tspec = pl.BlockSpec((tm, LANES), lambda b, i: (i, 0))
    return pl.pallas_call(
        _proj_kernel,
        grid=(B, S // tm),
        in_specs=[pl.BlockSpec((1, tm, D), lambda b, i: (b, i, 0)),
                  _const_spec(gmix.shape), _const_spec(win.shape), _const_spec(gcq.shape),
                  _const_spec(wuq.shape), _const_spec(gckv.shape), _const_spec(wukv.shape),
                  _const_spec(gqn.shape), _const_spec(gkn.shape), tspec, tspec, tspec],
        out_specs=[hspec, hspec, hspec, uspec, uspec],
        out_shape=[hs, hs, hs, us, us],
        compiler_params=_cparams(("parallel", "parallel")),
        name="mla_rglru_in_proj",
    )(x, gmix, win, gcq, wuq, gckv, wukv, gqn, gkn, cos_t, s1_t, s2_t)


def _rglru_kernel(ug_ref, ux_ref, cw_ref, cb_ref, wrg_ref, brg_ref, wig_ref, big_ref, lam_ref, gout_ref,
                  o_ref, ubuf, a_s, b_s, h_s, hc_s, *, tm):
    W = RNN_WIDTH

    @pl.when(pl.program_id(1) == 0)
    def _():
        ubuf[0:8, :] = jnp.zeros((8, W), F32)
        hc_s[...] = jnp.zeros((8, W), F32)

    ubuf[8:tm + 8, :] = ux_ref[0]
    cw = cw_ref[...]
    xc = cb_ref[...] + sum(ubuf[pl.ds(8 - (CONV_WIDTH - 1) + j, tm), :] * cw[j:j + 1, :] for j in range(CONV_WIDTH))
    ubuf[0:8, :] = ubuf[tm:tm + 8, :]
    r = jax.nn.sigmoid(_bdot(xc, wrg_ref[...]) + brg_ref[...])
    i = jax.nn.sigmoid(_bdot(xc, wig_ref[...]) + big_ref[...])
    nl = -lam_ref[...]
    softplus = jnp.maximum(nl, 0.0) + jnp.log(1.0 + jnp.exp(-jnp.abs(nl)))
    a = jnp.exp((-LRU_C * r) * softplus)
    bt = jnp.sqrt(1.0 - a * a) * (i * xc)

    a3 = a.reshape(tm // 8, 8, W)
    b3 = bt.reshape(tm // 8, 8, W)
    row = lax.broadcasted_iota(jnp.int32, (tm // 8, 8, W), 1)
    for d in (1, 2, 4):
        keep = row >= d
        a_sh = jnp.where(keep, pltpu.roll(a3, d, axis=1), 1.0)
        b_sh = jnp.where(keep, pltpu.roll(b3, d, axis=1), 0.0)
        b3 = a3 * b_sh + b3
        a3 = a3 * a_sh
    a_s[...] = a3.reshape(tm, W)
    b_s[...] = b3.reshape(tm, W)

    def body(g, hprev):
        sl = pl.ds(pl.multiple_of(g * 8, 8), 8)
        hr = a_s[sl, :] * hprev + b_s[sl, :]
        h_s[sl, :] = hr
        return jnp.broadcast_to(hr[7:8, :], (8, W))

    hc_s[...] = lax.fori_loop(0, tm // 8, body, hc_s[...])
    o = jax.nn.gelu(ug_ref[0], approximate=True) * h_s[...]
    o_ref[0] = _rms(o, gout_ref[...]).astype(BF16)


def _rglru_call(ug, ux, cw, cb, wrg, brg, wig, big, lam, gout, tm):
    B, S, W = ux.shape
    uspec = pl.BlockSpec((1, tm, W), lambda b, i: (b, i, 0))
    consts = (cw, cb, wrg, brg, wig, big, lam, gout)
    return pl.pallas_call(
        functools.partial(_rglru_kernel, tm=tm),
        grid=(B, S // tm),
        in_specs=[uspec, uspec] + [_const_spec(c.shape) for c in consts],
        out_specs=uspec,
        out_shape=jax.ShapeDtypeStruct((B, S, W), BF16),
        scratch_shapes=[pltpu.VMEM((tm + 8, W), F32), pltpu.VMEM((tm, W), F32), pltpu.VMEM((tm, W), F32),
                        pltpu.VMEM((tm, W), F32), pltpu.VMEM((8, W), F32)],
        compiler_params=_cparams(("parallel", "arbitrary")),
        name="rglru",
    )(ug, ux, *consts)


def _attn_kernel(q_ref, k_ref, v_ref, bias_ref, o_ref, m_s, acc_s, *, tq, tk, hp):
    qi = pl.program_id(2)
    nsub = tq // tk
    n_full = qi * nsub
    m_s[...] = jnp.full(m_s.shape, NEG, F32)
    acc_s[...] = jnp.zeros(acc_s.shape, F32)

    def block(hh, kb, r0, nrows, masked):
        ks = pl.multiple_of(kb * tk, tk)
        rows = slice(r0, r0 + nrows)
        s = lax.dot_general(q_ref[0, hh, rows, :], k_ref[0, hh, pl.ds(ks, tk), :],
                            (((1,), (1,)), ((), ())), preferred_element_type=F32)
        if masked:
            s = s + bias_ref[...]
        m_prev = m_s[hh, rows, :]
        m_next = jnp.maximum(m_prev, jnp.max(s, axis=1, keepdims=True))
        alpha = jnp.exp2(m_prev - m_next)
        p = jnp.exp2(s - jnp.concatenate([m_next] * (tk // LANES), axis=1))
        pv = jnp.dot(p.astype(BF16), v_ref[0, hh, pl.ds(ks, tk), :], preferred_element_type=F32)
        acc_s[hh, rows, :] = acc_s[hh, rows, :] * alpha + pv
        m_s[hh, rows, :] = m_next

    def full_body(kb, carry):
        for hh in range(hp):
            block(hh, kb, 0, tq, False)
        return carry

    lax.fori_loop(0, n_full, full_body, 0)
    for j in range(nsub):
        for hh in range(hp):
            if j + 1 < nsub:
                block(hh, n_full + j, (j + 1) * tk, tq - (j + 1) * tk, False)
            block(hh, n_full + j, j * tk, tk, True)
    lane = lax.broadcasted_iota(jnp.int32, (tq, LANES), 1)
    for pr in range(hp // 2):
        a0, a1 = acc_s[2 * pr], acc_s[2 * pr + 1]
        o0 = a0 / a0[:, MLA_V:MLA_V + 1]
        o1 = a1 / a1[:, MLA_V:MLA_V + 1]
        o_ref[0, :, pr * LANES:(pr + 1) * LANES] = jnp.where(
            lane < MLA_V, o0, pltpu.roll(o1, MLA_V, axis=1)).astype(BF16)


def _attn_call(q, k, v, tq, tk, hp):
    B, H, S, _ = q.shape
    idx = jnp.arange(tk, dtype=jnp.int32) >> CHUNK_SHIFT
    bias = jnp.where(idx[None, :] <= idx[:, None], 0.0, NEG).astype(F32)
    return pl.pallas_call(
        functools.partial(_attn_kernel, tq=tq, tk=tk, hp=hp),
        grid=(B, H // hp, S // tq),
        in_specs=[pl.BlockSpec((1, hp, tq, LANES), lambda b, h, i: (b, h, i, 0)),
                  pl.BlockSpec((1, hp, S, LANES), lambda b, h, i: (b, h, 0, 0)),
                  pl.BlockSpec((1, hp, S, LANES), lambda b, h, i: (b, h, 0, 0)),
                  _const_spec(bias.shape)],
        out_specs=pl.BlockSpec((1, tq, hp * MLA_V), lambda b, h, i: (b, i, h)),
        out_shape=jax.ShapeDtypeStruct((B, S, H * MLA_V), BF16),
        scratch_shapes=[pltpu.VMEM((hp, tq, LANES), F32), pltpu.VMEM((hp, tq, LANES), F32)],
        compiler_params=_cparams(("parallel", "parallel", "arbitrary")),
        name="block_causal_attention",
    )(q, k, v, bias)


def _memkv_kernel(mem_ref, gmem_ref, wmk_ref, wmv_ref, gmkn_ref, k_ref, v_ref):
    m = _rms(mem_ref[0], gmem_ref[...]).astype(BF16)
    k = jnp.dot(m, wmk_ref[...], preferred_element_type=F32)
    v = jnp.dot(m, wmv_ref[...], preferred_element_type=F32)
    g = gmkn_ref[...]
    for h in range(MEM_HEADS):
        sl = slice(h * MEM_HEAD_DIM, (h + 1) * MEM_HEAD_DIM)
        k_ref[0, :, sl] = _rms(k[:, sl], g).astype(BF16)
    v_ref[0] = v.astype(BF16)


def _memkv_call(mem, gmem, wmk, wmv, gmkn):
    B, M, D = mem.shape
    spec = pl.BlockSpec((1, M, D), lambda b: (b, 0, 0))
    os_ = jax.ShapeDtypeStruct((B, M, D), BF16)
    return pl.pallas_call(
        _memkv_kernel,
        grid=(B,),
        in_specs=[spec, _const_spec(gmem.shape), _const_spec(wmk.shape), _const_spec(wmv.shape),
                  _const_spec(gmkn.shape)],
        out_specs=[spec, spec],
        out_shape=[os_, os_],
        compiler_params=_cparams(("parallel",)),
        name="mem_kv",
    )(mem, gmem, wmk, wmv, gmkn)


def _mix_mem_kernel(x_ref, oa_ref, orn_ref, gao_ref, wout_ref, gxq_ref, wmq_ref, gmqn_ref, km_ref, vm_ref,
                    wmo_ref, o_ref):
    oa = _rms(oa_ref[0].astype(F32), gao_ref[...]).astype(BF16)
    mix = jnp.concatenate([oa, orn_ref[0]], axis=-1)
    x1 = x_ref[0] + jnp.dot(mix, wout_ref[...], preferred_element_type=F32)
    qm = _bdot(_rms(x1, gxq_ref[...]), wmq_ref[...])
    g = gmqn_ref[...]
    scale = MEM_HEAD_DIM ** -0.5
    heads = []
    for h in range(MEM_HEADS):
        sl = slice(h * MEM_HEAD_DIM, (h + 1) * MEM_HEAD_DIM)
        qh = (_rms(qm[:, sl], g) * scale).astype(BF16)
        s = lax.dot_general(qh, km_ref[0, :, sl], (((1,), (1,)), ((), ())), preferred_element_type=F32)
        p = jnp.exp(s - jnp.max(s, axis=-1, keepdims=True))
        oh = jnp.dot(p.astype(BF16), vm_ref[0, :, sl], preferred_element_type=F32)
        heads.append((oh / jnp.sum(p, axis=-1, keepdims=True)).astype(BF16))
    o_ref[0] = x1 + jnp.dot(jnp.concatenate(heads, axis=-1), wmo_ref[...], preferred_element_type=F32)


def _mix_mem_call(x, oa, orn, gao, wout, gxq, wmq, gmqn, km, vm, wmo, tm):
    B, S, D = x.shape
    M = km.shape[1]
    xspec = pl.BlockSpec((1, tm, D), lambda b, i: (b, i, 0))
    hspec = pl.BlockSpec((1, tm, D // 2), lambda b, i: (b, i, 0))
    mspec = pl.BlockSpec((1, M, D), lambda b, i: (b, 0, 0))
    return pl.pallas_call(
        _mix_mem_kernel,
        grid=(B, S // tm),
        in_specs=[xspec, hspec, hspec, _const_spec(gao.shape), _const_spec(wout.shape), _const_spec(gxq.shape),
                  _const_spec(wmq.shape), _const_spec(gmqn.shape), mspec, mspec, _const_spec(wmo.shape)],
        out_specs=xspec,
        out_shape=jax.ShapeDtypeStruct((B, S, D), F32),
        compiler_params=_cparams(("parallel", "parallel")),
        name="out_proj_mem_attn",
    )(x, oa, orn, gao, wout, gxq, wmq, gmqn, km, vm, wmo)


def _route_t(lt):
    tm = lt.shape[1]
    srow = lax.broadcasted_iota(jnp.int32, (EXPERTS_PER_GROUP, tm), 0)
    glog = jnp.where(srow < N_GROUPS, lt[0:8], NEG)
    gmax = jnp.max(glog, axis=0, keepdims=True)
    p_g = 1.0 / jnp.sum(jnp.exp(glog - gmax), axis=0, keepdims=True)
    g_idx = jnp.min(jnp.where(glog == gmax, srow, 8), axis=0, keepdims=True)
    el = lt[ROUTER_OFF:ROUTER_OFF + EXPERTS_PER_GROUP]
    for gg in range(1, N_GROUPS):
        lo = ROUTER_OFF + gg * EXPERTS_PER_GROUP
        el = jnp.where(g_idx == gg, lt[lo:lo + EXPERTS_PER_GROUP], el)
    ee = jnp.exp(el - jnp.max(el, axis=0, keepdims=True))
    probs = ee / jnp.sum(ee, axis=0, keepdims=True)
    v1 = jnp.max(probs, axis=0, keepdims=True)
    i1 = jnp.min(jnp.where(probs == v1, srow, 8), axis=0, keepdims=True)
    rest = srow != i1
    v2 = jnp.max(jnp.where(rest, probs, -1.0), axis=0, keepdims=True)
    i2 = jnp.min(jnp.where(rest & (probs == v2), srow, 8), axis=0, keepdims=True)
    tot = v1 + v2
    gl = jnp.where(srow == i1, p_g * (v1 / tot), 0.0) + jnp.where(srow == i2, p_g * (v2 / tot), 0.0)
    return gl, g_idx


def _moe_kernel(x_ref, gffn_ref, wr_ref, br_ref, wgu_ref, wd_ref, o_ref, text_s, metac_s, metar_s, cnt_s, *, ch):
    g = pl.program_id(1)
    tm, D = x_ref.shape

    @pl.when(g == 0)
    def _():
        x = x_ref[...]
        t = _rms(x, gffn_ref[...])
        thi = t.astype(BF16)
        tlo = (t - thi.astype(F32)).astype(BF16)
        l1 = jnp.dot(thi, wr_ref[...], preferred_element_type=F32)
        l2 = jnp.dot(tlo, wr_ref[:, 0:LANES], preferred_element_type=F32)
        logits = l1[:, 0:LANES] + l1[:, LANES:] + l2 + br_ref[...]
        gl, gid = _route_t(logits.T)
        srow = lax.broadcasted_iota(jnp.int32, (8, tm), 0)
        oh = srow == gid
        upper = lax.broadcasted_iota(jnp.int32, (tm, tm), 0) < lax.broadcasted_iota(jnp.int32, (tm, tm), 1)
        pref = jnp.dot(oh.astype(BF16), upper.astype(BF16), preferred_element_type=F32)
        rank = jnp.sum(jnp.where(oh, pref, 0.0), axis=0, keepdims=True)
        rg = jnp.where(srow == 0, rank, jnp.where(srow == 1, gid.astype(F32), 0.0))
        metar_s[...] = rg
        info = jnp.concatenate([gl, rg, jnp.zeros((LANES - 16, tm), F32)], axis=0).T
        metac_s[...] = info
        ghi = info.astype(BF16)
        text_s[:, 0:D] = thi
        text_s[:, D:D + LANES] = ghi
        text_s[:, D + LANES:] = (info - ghi.astype(F32)).astype(BF16)
        for gg in range(N_GROUPS):
            cnt_s[gg] = jnp.sum(oh[gg:gg + 1, :].astype(jnp.int32))
        o_ref[...] = x

    gf = g.astype(F32)
    nch = (cnt_s[g] + (ch - 1)) // ch
    key_row = jnp.where(metar_s[1:2, :] == gf, metar_s[0:1, :], -1.0)
    mc = metac_s[...]
    key_col = jnp.where(mc[:, 9:10] == gf, mc[:, 8:9], -1.0)
    sub = lax.broadcasted_iota(jnp.int32, (ch, tm), 0).astype(F32)
    lan = lax.broadcasted_iota(jnp.int32, (tm, ch), 1).astype(F32)

    def chunk(c, carry):
        base = (c * ch).astype(F32)
        disp = (key_row - base == sub).astype(BF16)
        xc = jnp.dot(disp, text_s[...], preferred_element_type=F32)
        xb = xc[:, :D].astype(BF16)
        gat = xc[:, D:D + LANES] + xc[:, D + LANES:]
        y = jnp.zeros((ch, D), F32)
        for e in range(EXPERTS_PER_GROUP):
            h = jnp.dot(xb, wgu_ref[e], preferred_element_type=F32)
            he = jax.nn.silu(h[:, :D_EXPERT]) * h[:, D_EXPERT:] * gat[:, e:e + 1]
            y = y + _bdot(he, wd_ref[e])
        comb = (key_col - base == lan).astype(BF16)
        o_ref[...] += jnp.dot(comb, y.astype(BF16), preferred_element_type=F32)
        return carry

    lax.fori_loop(0, nch, chunk, 0)


def _moe_call(x2, gffn, wr, br, wgu, wd, tm, ch):
    N, D = x2.shape
    xspec = pl.BlockSpec((tm, D), lambda i, g: (i, 0))
    epg = EXPERTS_PER_GROUP
    return pl.pallas_call(
        functools.partial(_moe_kernel, ch=ch),
        grid=(N // tm, N_GROUPS),
        in_specs=[xspec, _const_spec(gffn.shape), _const_spec(wr.shape), _const_spec(br.shape),
                  pl.BlockSpec((epg, D, 2 * D_EXPERT), lambda i, g: (g, 0, 0)),
                  pl.BlockSpec((epg, D_EXPERT, D), lambda i, g: (g, 0, 0))],
        out_specs=xspec,
        out_shape=jax.ShapeDtypeStruct((N, D), F32),
        scratch_shapes=[pltpu.VMEM((tm, D + 2 * LANES), BF16), pltpu.VMEM((tm, LANES), F32),
                        pltpu.VMEM((8, tm), F32), pltpu.SMEM((N_GROUPS,), jnp.int32)],
        compiler_params=_cparams(("parallel", "arbitrary")),
        name="hier_moe_grouped",
    )(x2, gffn, wr, br, wgu, wd)


def _pad_lanes(w, width):
    return jnp.pad(w, [(0, 0)] * (w.ndim - 1) + [(0, width - w.shape[-1])])


def _rope_tables(S):
    pos = jnp.arange(S, dtype=F32)
    inv_freq = ROPE_BASE ** (-jnp.arange(0, MLA_ROPE, 2, dtype=F32) / MLA_ROPE)
    ang = pos[:, None] * inv_freq[None, :]
    cos, sin = jnp.cos(ang), jnp.sin(ang)
    half = MLA_ROPE // 2
    z = lambda n: jnp.zeros((S, n), F32)
    cos_t = jnp.concatenate([jnp.ones((S, MLA_NOPE), F32), cos, cos, z(LANES - MLA_QK)], axis=1)
    s1_t = jnp.concatenate([z(MLA_NOPE), -sin, z(LANES - MLA_NOPE - half)], axis=1)
    s2_t = jnp.concatenate([z(MLA_NOPE + half), sin, z(LANES - MLA_QK)], axis=1)
    return cos_t, s1_t, s2_t


def _block_diag(w):
    n, c, d = w.shape
    eye = jnp.eye(n, dtype=w.dtype)
    return (eye[:, None, :, None] * w[:, :, None, :]).reshape(n * c, n * d)


def kernel(x, mem, g_mix, w_in, g_cq, w_uq, g_ckv, w_ukv, g_qn, g_kn, conv_w, conv_b, w_rg, b_rg, w_ig, b_ig,
           lam, g_attn_out, g_rnn_out, w_out, g_xq, g_mem, w_mq, w_mk, w_mv, g_mqn, g_mkn, w_mo, g_ffn,
           w_group, b_group, w_expert, b_expert, w_e_gate, w_e_up, w_e_down):
    B, S, D = x.shape
    H = MLA_HEADS
    tm = min(512, S)
    tk = min(512, S)
    tq = min(1024, S)
    row = lambda a: a.reshape(1, -1)
    cos_t, s1_t, s2_t = _rope_tables(S)
    for l in range(g_mix.shape[0]):
        wi = w_in[l]
        c0, c1, c2 = Q_LORA + KV_LORA, Q_LORA + KV_LORA + MLA_ROPE, Q_LORA + KV_LORA + MLA_ROPE + RNN_WIDTH
        zc = lambda n: jnp.zeros((D, n), wi.dtype)
        win = jnp.concatenate([wi[:, :c0], zc(MLA_NOPE), wi[:, c0:c1], zc(LANES - MLA_QK), wi[:, c1:c2], wi[:, c2:]],
                              axis=1).astype(BF16)
        wuq = _pad_lanes(w_uq[l].reshape(Q_LORA, H, MLA_QK), LANES).reshape(Q_LORA, H * LANES).astype(BF16)
        wkv = w_ukv[l].reshape(KV_LORA, H, MLA_NOPE + MLA_V)
        wukv = jnp.concatenate([_pad_lanes(wkv[:, :, :MLA_NOPE], LANES).reshape(KV_LORA, H * LANES),
                                _pad_lanes(wkv[:, :, MLA_NOPE:], LANES).reshape(KV_LORA, H * LANES)],
                               axis=1).astype(BF16)
        gqn = _pad_lanes(row(g_qn[l]), LANES)
        gkn = _pad_lanes(row(g_kn[l]), LANES)
        wrg = _block_diag(w_rg[l]).astype(BF16)
        wig = _block_diag(w_ig[l]).astype(BF16)
        gpad = ROUTER_OFF - N_GROUPS
        wr32 = _pad_lanes(jnp.concatenate([_pad_lanes(w_group[l], ROUTER_OFF), w_expert[l]], axis=1), LANES)
        wr_hi = wr32.astype(BF16)
        wr = jnp.concatenate([wr_hi, (wr32 - wr_hi.astype(F32)).astype(BF16)], axis=1)
        br = _pad_lanes(row(jnp.concatenate([b_group[l], jnp.zeros((gpad,), F32), b_expert[l]])), LANES)
        wgu = jnp.concatenate([w_e_gate[l], w_e_up[l]], axis=-1).astype(BF16)
        wd = w_e_down[l].astype(BF16)

        q, k, v, ug, ux = _proj_call(x, row(g_mix[l]), win, row(g_cq[l]), wuq, row(g_ckv[l]), wukv, gqn, gkn,
                                     cos_t, s1_t, s2_t, tm)
        o_rnn = _rglru_call(ug, ux, conv_w[l], row(conv_b[l]), wrg, row(b_rg[l]), wig, row(b_ig[l]), row(lam[l]),
                            row(g_rnn_out[l]), tm)
        o_attn = _attn_call(q, k, v, tq, tk, 2)
        km, vm = _memkv_call(mem, row(g_mem[l]), w_mk[l].astype(BF16), w_mv[l].astype(BF16), row(g_mkn[l]))
        x2 = _mix_mem_call(x, o_attn, o_rnn, row(g_attn_out[l]), w_out[l].astype(BF16), row(g_xq[l]),
                           w_mq[l].astype(BF16), row(g_mqn[l]), km, vm, w_mo[l].astype(BF16), tm)
        x = _moe_call(x2.reshape(B * S, D), row(g_ffn[l]), wr, br, wgu, wd, min(1024, B * S), MOE_CHUNK).reshape(B, S, D)
    return x
```

```python
import functools

import jax
import jax.numpy as jnp
from jax import lax
from jax.experimental import pallas as pl
from jax.experimental.pallas import tpu as pltpu

F32 = jnp.float32
BF16 = jnp.bfloat16

EPS = 1e-6
LANES = 128
CHUNK_SHIFT = 6
MLA_HEADS = 8
MLA_NOPE = 64
MLA_ROPE = 32
MLA_QK = MLA_NOPE + MLA_ROPE
MLA_V = 64
Q_LORA = 256
KV_LORA = 128
RNN_WIDTH = 512
RNN_BLOCKS = 8
CONV_WIDTH = 4
LRU_C = 8.0
ROPE_BASE = 10000.0
MEM_HEADS = 4
MEM_HEAD_DIM = 256
N_GROUPS = 4
EXPERTS_PER_GROUP = 8
N_EXPERTS = 32
D_EXPERT = 256
ROUTER_OFF = 8
MOE_CHUNK = 320
NEG = -1e30
LOG2E = 1.4426950408889634
VMEM_LIMIT = 56 * 1024 * 1024


def _rms(x, g):
    return x * lax.rsqrt(jnp.mean(x * x, axis=-1, keepdims=True) + EPS) * g


def _bdot(a, b):
    return jnp.dot(a.astype(BF16), b, preferred_element_type=F32)


def _cparams(sem):
    return pltpu.CompilerParams(dimension_semantics=sem, vmem_limit_bytes=VMEM_LIMIT)


def _const_spec(shape):
    return pl.BlockSpec(shape, lambda *_: (0,) * len(shape))


def _proj_kernel(x_ref, gmix_ref, win_ref, gcq_ref, wuq_ref, gckv_ref, wukv_ref, gkn_ref,
                 tq_ref, ck_ref, s1k_ref, s2k_ref, q_ref, k_ref, v_ref, ug_ref, ux_ref, *, nsplit):
    tm = x_ref.shape[1]
    rs = tm // nsplit
    lane = lax.broadcasted_iota(jnp.int32, (1, LANES), 1)
    ones_col = (lane == MLA_V).astype(F32)
    qmask = (lane < MLA_QK).astype(F32)
    inv = 1.0 / MLA_QK
    half = MLA_ROPE // 2
    gkn = gkn_ref[...]
    for part in range(nsplit):
        rows = slice(part * rs, (part + 1) * rs)
        z = _bdot(_rms(x_ref[0, rows, :], gmix_ref[...]), win_ref[...])
        ug_ref[0, rows, :] = z[:, 512:1024]
        ux_ref[0, rows, :] = z[:, 1024:1536]
        q = _bdot(_rms(z[:, 0:Q_LORA], gcq_ref[...]), wuq_ref[...])
        kv = _bdot(_rms(z[:, Q_LORA:Q_LORA + KV_LORA], gckv_ref[...]), wukv_ref[...])
        kr = z[:, 384:512]
        ss_rope = jnp.sum(kr * kr, axis=-1, keepdims=True)
        rot = (kr * ck_ref[rows, :] + pltpu.roll(kr, LANES - half, axis=1) * s1k_ref[rows, :]
               + pltpu.roll(kr, half, axis=1) * s2k_ref[rows, :])
        rot2 = rot + pltpu.roll(rot, MLA_ROPE, axis=1)
        tq = tq_ref[rows, :]
        for h in range(MLA_HEADS):
            sl = slice(h * LANES, (h + 1) * LANES)
            qh = q[:, sl]
            rq = lax.rsqrt(jnp.sum(qh * qh * qmask, axis=-1, keepdims=True) * inv + EPS)
            q_ref[0, h, rows, :] = (qh * rq * tq).astype(BF16)
            kn = kv[:, sl]
            rk = lax.rsqrt((jnp.sum(kn * kn, axis=-1, keepdims=True) + ss_rope) * inv + EPS)
            k_ref[0, h, rows, :] = ((kn * gkn + rot2) * rk).astype(BF16)
            vsl = slice((MLA_HEADS + h) * LANES, (MLA_HEADS + h + 1) * LANES)
            v_ref[0, h, rows, :] = (kv[:, vsl] + ones_col).astype(BF16)


def _proj_call(x, gmix, win, gcq, wuq, gckv, wukv, gkn, tabs, tm):
    B, S, D = x.shape
    H = MLA_HEADS
    hs = jax.ShapeDtypeStruct((B, H, S, LANES), BF16)
    us = jax.ShapeDtypeStruct((B, S, RNN_WIDTH), F32)
    hspec = pl.BlockSpec((1, H, tm, LANES), lambda b, i: (b, 0, i, 0))
    uspec = pl.BlockSpec((1, tm, RNN_WIDTH), lambda b, i: (b, i, 0))
    tspec = pl.BlockSpec((tm, LANES), lambda b, i: (i, 0))
    return pl.pallas_call(
        functools.partial(_proj_kernel, nsplit=2 if tm % 16 == 0 else 1),
        grid=(B, S // tm),
        in_specs=[pl.BlockSpec((1, tm, D), lambda b, i: (b, i, 0)),
                  _const_spec(gmix.shape), _const_spec(win.shape), _const_spec(gcq.shape),
                  _const_spec(wuq.shape), _const_spec(gckv.shape), _const_spec(wukv.shape),
                  _const_spec(gkn.shape), tspec, tspec, tspec, tspec],
        out_specs=[hspec, hspec, hspec, uspec, uspec],
        out_shape=[hs, hs, hs, us, us],
        compiler_params=_cparams(("parallel", "parallel")),
        name="mla_rglru_in_proj",
    )(x, gmix, win, gcq, wuq, gckv, wukv, gkn, *tabs)


def _rglru_kernel(ug_ref, ux_ref, cw_ref, cb_ref, wrg_ref, brg_ref, wig_ref, big_ref, lam_ref, gout_ref,
                  o_ref, ubuf, a_s, b_s, h_s, hc_s, *, tm):
    W = RNN_WIDTH

    @pl.when(pl.program_id(1) == 0)
    def _():
        ubuf[0:8, :] = jnp.zeros((8, W), F32)
        hc_s[...] = jnp.zeros((8, W), F32)

    ubuf[8:tm + 8, :] = ux_ref[0]
    cw = cw_ref[...]
    xc = cb_ref[...] + sum(ubuf[pl.ds(8 - (CONV_WIDTH - 1) + j, tm), :] * cw[j:j + 1, :] for j in range(CONV_WIDTH))
    ubuf[0:8, :] = ubuf[tm:tm + 8, :]
    r = jax.nn.sigmoid(_bdot(xc, wrg_ref[...]) + brg_ref[...])
    i = jax.nn.sigmoid(_bdot(xc, wig_ref[...]) + big_ref[...])
    nl = -lam_ref[...]
    softplus = jnp.maximum(nl, 0.0) + jnp.log(1.0 + jnp.exp(-jnp.abs(nl)))
    a = jnp.exp((-LRU_C * r) * softplus)
    bt = jnp.sqrt(1.0 - a * a) * (i * xc)

    a3 = a.reshape(tm // 8, 8, W)
    b3 = bt.reshape(tm // 8, 8, W)
    row = lax.broadcasted_iota(jnp.int32, (tm // 8, 8, W), 1)
    for d in (1, 2, 4):
        keep = row >= d
        a_sh = jnp.where(keep, pltpu.roll(a3, d, axis=1), 1.0)
        b_sh = jnp.where(keep, pltpu.roll(b3, d, axis=1), 0.0)
        b3 = a3 * b_sh + b3
        a3 = a3 * a_sh
    a_s[...] = a3.reshape(tm, W)
    b_s[...] = b3.reshape(tm, W)

    def body(g, hprev):
        sl = pl.ds(pl.multiple_of(g * 8, 8), 8)
        hr = a_s[sl, :] * hprev + b_s[sl, :]
        h_s[sl, :] = hr
        return jnp.broadcast_to(hr[7:8, :], (8, W))

    hc_s[...] = lax.fori_loop(0, tm // 8, body, hc_s[...])
    o = jax.nn.gelu(ug_ref[0], approximate=True) * h_s[...]
    o_ref[0] = _rms(o, gout_ref[...]).astype(BF16)


def _rglru_call(ug, ux, cw, cb, wrg, brg, wig, big, lam, gout, tm):
    B, S, W = ux.shape
    uspec = pl.BlockSpec((1, tm, W), lambda b, i: (b, i, 0))
    consts = (cw, cb, wrg, brg, wig, big, lam, gout)
    return pl.pallas_call(
        functools.partial(_rglru_kernel, tm=tm),
        grid=(B, S // tm),
        in_specs=[uspec, uspec] + [_const_spec(c.shape) for c in consts],
        out_specs=uspec,
        out_shape=jax.ShapeDtypeStruct((B, S, W), BF16),
        scratch_shapes=[pltpu.VMEM((tm + 8, W), F32), pltpu.VMEM((tm, W), F32), pltpu.VMEM((tm, W), F32),
                        pltpu.VMEM((tm, W), F32), pltpu.VMEM((8, W), F32)],
        compiler_params=_cparams(("parallel", "arbitrary")),
        name="rglru",
    )(ug, ux, *consts)


def _attn_kernel(q_ref, k_ref, v_ref, bias_ref, o_ref, m_s, acc_s, *, tq, tk, hp):
    qi = pl.program_id(2)
    nsub = tq // tk
    m_s[...] = jnp.full(m_s.shape, NEG, F32)
    acc_s[...] = jnp.zeros(acc_s.shape, F32)

    def block(hh, k0, ksz, r0, nrows, masked):
        rows = slice(r0, r0 + nrows)
        s = lax.dot_general(q_ref[0, hh, rows, :], k_ref[0, hh, pl.ds(k0, ksz), :],
                            (((1,), (1,)), ((), ())), preferred_element_type=F32)
        if masked:
            tail = s[:, ksz - tk:] + bias_ref[...]
            s = tail if ksz == tk else jnp.concatenate([s[:, :ksz - tk], tail], axis=1)
        m_prev = m_s[hh, rows, :]
        m_next = jnp.maximum(m_prev, jnp.max(s, axis=1, keepdims=True))
        alpha = jnp.exp2(m_prev - m_next)
        p = jnp.exp2(s - jnp.concatenate([m_next] * (ksz // LANES), axis=1))
        pv = jnp.dot(p.astype(BF16), v_ref[0, hh, pl.ds(k0, ksz), :], preferred_element_type=F32)
        acc_s[hh, rows, :] = acc_s[hh, rows, :] * alpha + pv
        m_s[hh, rows, :] = m_next

    def full_body(kb, carry):
        k0 = pl.multiple_of(kb * tq, tq)
        for hh in range(hp):
            block(hh, k0, tq, 0, tq, False)
        return carry

    lax.fori_loop(0, qi, full_body, 0)
    k0 = pl.multiple_of(qi * tq, tq)
    for j in range(nsub):
        for hh in range(hp):
            block(hh, k0, (j + 1) * tk, j * tk, tk, True)
    lane = lax.broadcasted_iota(jnp.int32, (tq, LANES), 1)
    for pr in range(hp // 2):
        a0, a1 = acc_s[2 * pr], acc_s[2 * pr + 1]
        o0 = a0 / a0[:, MLA_V:MLA_V + 1]
        o1 = a1 / a1[:, MLA_V:MLA_V + 1]
        o_ref[0, :, pr * LANES:(pr + 1) * LANES] = jnp.where(
            lane < MLA_V, o0, pltpu.roll(o1, MLA_V, axis=1)).astype(BF16)


def _attn_call(q, k, v, tq, tk, hp):
    B, H, S, _ = q.shape
    idx = jnp.arange(tk, dtype=jnp.int32) >> CHUNK_SHIFT
    bias = jnp.where(idx[None, :] <= idx[:, None], 0.0, NEG).astype(F32)
    return pl.pallas_call(
        functools.partial(_attn_kernel, tq=tq, tk=tk, hp=hp),
        grid=(B, H // hp, S // tq),
        in_specs=[pl.BlockSpec((1, hp, tq, LANES), lambda b, h, i: (b, h, i, 0)),
                  pl.BlockSpec((1, hp, S, LANES), lambda b, h, i: (b, h, 0, 0)),
                  pl.BlockSpec((1, hp, S, LANES), lambda b, h, i: (b, h, 0, 0)),
                  _const_spec(bias.shape)],
        out_specs=pl.BlockSpec((1, tq, hp * MLA_V), lambda b, h, i: (b, i, h)),
        out_shape=jax.ShapeDtypeStruct((B, S, H * MLA_V), BF16),
        scratch_shapes=[pltpu.VMEM((hp, tq, LANES), F32), pltpu.VMEM((hp, tq, LANES), F32)],
        compiler_params=_cparams(("parallel", "parallel", "arbitrary")),
        name="block_causal_attention",
    )(q, k, v, bias)


def _memkv_kernel(mem_ref, gmem_ref, wmk_ref, wmv_ref, gmkn_ref, k_ref, v_ref):
    m = _rms(mem_ref[0], gmem_ref[...]).astype(BF16)
    k = jnp.dot(m, wmk_ref[...], preferred_element_type=F32)
    v = jnp.dot(m, wmv_ref[...], preferred_element_type=F32)
    g = gmkn_ref[...]
    for h in range(MEM_HEADS):
        sl = slice(h * MEM_HEAD_DIM, (h + 1) * MEM_HEAD_DIM)
        k_ref[0, :, sl] = _rms(k[:, sl], g).astype(BF16)
    v_ref[0] = v.astype(BF16)


def _memkv_call(mem, gmem, wmk, wmv, gmkn):
    B, M, D = mem.shape
    spec = pl.BlockSpec((1, M, D), lambda b: (b, 0, 0))
    os_ = jax.ShapeDtypeStruct((B, M, D), BF16)
    return pl.pallas_call(
        _memkv_kernel,
        grid=(B,),
        in_specs=[spec, _const_spec(gmem.shape), _const_spec(wmk.shape), _const_spec(wmv.shape),
                  _const_spec(gmkn.shape)],
        out_specs=[spec, spec],
        out_shape=[os_, os_],
        compiler_params=_cparams(("parallel",)),
        name="mem_kv",
    )(mem, gmem, wmk, wmv, gmkn)


def _mix_mem_kernel(x_ref, oa_ref, orn_ref, gao_ref, wout_ref, gxq_ref, wmq_ref, gmqn_ref, km_ref, vm_ref,
                    wmo_ref, o_ref):
    oa = _rms(oa_ref[0].astype(F32), gao_ref[...]).astype(BF16)
    mix = jnp.concatenate([oa, orn_ref[0]], axis=-1)
    x1 = x_ref[0] + jnp.dot(mix, wout_ref[...], preferred_element_type=F32)
    qm = _bdot(_rms(x1, gxq_ref[...]), wmq_ref[...])
    g = gmqn_ref[...]
    scale = MEM_HEAD_DIM ** -0.5
    heads = []
    for h in range(MEM_HEADS):
        sl = slice(h * MEM_HEAD_DIM, (h + 1) * MEM_HEAD_DIM)
        qh = (_rms(qm[:, sl], g) * scale).astype(BF16)
        s = lax.dot_general(qh, km_ref[0, :, sl], (((1,), (1,)), ((), ())), preferred_element_type=F32)
        p = jnp.exp(s - jnp.max(s, axis=-1, keepdims=True))
        oh = jnp.dot(p.astype(BF16), vm_ref[0, :, sl], preferred_element_type=F32)
        heads.append((oh / jnp.sum(p, axis=-1, keepdims=True)).astype(BF16))
    o_ref[0] = x1 + jnp.dot(jnp.concatenate(heads, axis=-1), wmo_ref[...], preferred_element_type=F32)


def _mix_mem_call(x, oa, orn, gao, wout, gxq, wmq, gmqn, km, vm, wmo, tm):
    B, S, D = x.shape
    M = km.shape[1]
    xspec = pl.BlockSpec((1, tm, D), lambda b, i: (b, i, 0))
    hspec = pl.BlockSpec((1, tm, D // 2), lambda b, i: (b, i, 0))
    mspec = pl.BlockSpec((1, M, D), lambda b, i: (b, 0, 0))
    return pl.pallas_call(
        _mix_mem_kernel,
        grid=(B, S // tm),
        in_specs=[xspec, hspec, hspec, _const_spec(gao.shape), _const_spec(wout.shape), _const_spec(gxq.shape),
                  _const_spec(wmq.shape), _const_spec(gmqn.shape), mspec, mspec, _const_spec(wmo.shape)],
        out_specs=xspec,
        out_shape=jax.ShapeDtypeStruct((B, S, D), F32),
        compiler_params=_cparams(("parallel", "parallel")),
        name="out_proj_mem_attn",
    )(x, oa, orn, gao, wout, gxq, wmq, gmqn, km, vm, wmo)


def _route_t(lt):
    tm = lt.shape[1]
    srow = lax.broadcasted_iota(jnp.int32, (EXPERTS_PER_GROUP, tm), 0)
    glog = jnp.where(srow < N_GROUPS, lt[0:8], NEG)
    gmax = jnp.max(glog, axis=0, keepdims=True)
    p_g = 1.0 / jnp.sum(jnp.exp(glog - gmax), axis=0, keepdims=True)
    g_idx = jnp.min(jnp.where(glog == gmax, srow, 8), axis=0, keepdims=True)
    el = lt[ROUTER_OFF:ROUTER_OFF + EXPERTS_PER_GROUP]
    for gg in range(1, N_GROUPS):
        lo = ROUTER_OFF + gg * EXPERTS_PER_GROUP
        el = jnp.where(g_idx == gg, lt[lo:lo + EXPERTS_PER_GROUP], el)
    ee = jnp.exp(el - jnp.max(el, axis=0, keepdims=True))
    probs = ee / jnp.sum(ee, axis=0, keepdims=True)
    v1 = jnp.max(probs, axis=0, keepdims=True)
    i1 = jnp.min(jnp.where(probs == v1, srow, 8), axis=0, keepdims=True)
    rest = srow != i1
    v2 = jnp.max(jnp.where(rest, probs, -1.0), axis=0, keepdims=True)
    i2 = jnp.min(jnp.where(rest & (probs == v2), srow, 8), axis=0, keepdims=True)
    tot = v1 + v2
    gl = jnp.where(srow == i1, p_g * (v1 / tot), 0.0) + jnp.where(srow == i2, p_g * (v2 / tot), 0.0)
    return gl, g_idx


def _moe_kernel(x_ref, gffn_ref, wr_ref, br_ref, wgu_ref, wd_ref, o_ref, text_s, metac_s, metar_s, cnt_s, *, ch):
    g = pl.program_id(1)
    tm, D = x_ref.shape

    @pl.when(g == 0)
    def _():
        x = x_ref[...]
        t = _rms(x, gffn_ref[...])
        thi = t.astype(BF16)
        tlo = (t - thi.astype(F32)).astype(BF16)
        l1 = jnp.dot(thi, wr_ref[...], preferred_element_type=F32)
        l2 = jnp.dot(tlo, wr_ref[:, 0:LANES], preferred_element_type=F32)
        logits = l1[:, 0:LANES] + l1[:, LANES:] + l2 + br_ref[...]
        gl, gid = _route_t(logits.T)
        srow = lax.broadcasted_iota(jnp.int32, (8, tm), 0)
        oh = srow == gid
        upper = lax.broadcasted_iota(jnp.int32, (tm, tm), 0) < lax.broadcasted_iota(jnp.int32, (tm, tm), 1)
        pref = jnp.dot(oh.astype(BF16), upper.astype(BF16), preferred_element_type=F32)
        rank = jnp.sum(jnp.where(oh, pref, 0.0), axis=0, keepdims=True)
        rg = jnp.where(srow == 0, rank, jnp.where(srow == 1, gid.astype(F32), 0.0))
        metar_s[...] = rg
        info = jnp.concatenate([gl, rg, jnp.zeros((LANES - 16, tm), F32)], axis=0).T
        metac_s[...] = info
        ghi = info.astype(BF16)
        text_s[:, 0:D] = thi
        text_s[:, D:D + LANES] = ghi
        text_s[:, D + LANES:] = (info - ghi.astype(F32)).astype(BF16)
        for gg in range(N_GROUPS):
            cnt_s[gg] = jnp.sum(oh[gg:gg + 1, :].astype(jnp.int32))
        o_ref[...] = x

    gf = g.astype(F32)
    nch = (cnt_s[g] + (ch - 1)) // ch
    key_row = jnp.where(metar_s[1:2, :] == gf, metar_s[0:1, :], -1.0)
    mc = metac_s[...]
    key_col = jnp.where(mc[:, 9:10] == gf, mc[:, 8:9], -1.0)
    sub = lax.broadcasted_iota(jnp.int32, (ch, tm), 0).astype(F32)
    lan = lax.broadcasted_iota(jnp.int32, (tm, ch), 1).astype(F32)

    def chunk(c, carry):
        base = (c * ch).astype(F32)
        disp = (key_row - base == sub).astype(BF16)
        xc = jnp.dot(disp, text_s[...], preferred_element_type=F32)
        xb = xc[:, :D].astype(BF16)
        gat = xc[:, D:D + LANES] + xc[:, D + LANES:]
        y = jnp.zeros((ch, D), F32)
        for e in range(EXPERTS_PER_GROUP):
            h = jnp.dot(xb, wgu_ref[e], preferred_element_type=F32)
            he = jax.nn.silu(h[:, :D_EXPERT]) * h[:, D_EXPERT:] * gat[:, e:e + 1]
            y = y + _bdot(he, wd_ref[e])
        comb = (key_col - base == lan).astype(BF16)
        o_ref[...] += jnp.dot(comb, y.astype(BF16), preferred_element_type=F32)
        return carry

    lax.fori_loop(0, nch, chunk, 0)


def _moe_call(x2, gffn, wr, br, wgu, wd, tm, ch):
    N, D = x2.shape
    xspec = pl.BlockSpec((tm, D), lambda i, g: (i, 0))
    epg = EXPERTS_PER_GROUP
    return pl.pallas_call(
        functools.partial(_moe_kernel, ch=ch),
        grid=(N // tm, N_GROUPS),
        in_specs=[xspec, _const_spec(gffn.shape), _const_spec(wr.shape), _const_spec(br.shape),
                  pl.BlockSpec((epg, D, 2 * D_EXPERT), lambda i, g: (g, 0, 0)),
                  pl.BlockSpec((epg, D_EXPERT, D), lambda i, g: (g, 0, 0))],
        out_specs=xspec,
        out_shape=jax.ShapeDtypeStruct((N, D), F32),
        scratch_shapes=[pltpu.VMEM((tm, D + 2 * LANES), BF16), pltpu.VMEM((tm, LANES), F32),
                        pltpu.VMEM((8, tm), F32), pltpu.SMEM((N_GROUPS,), jnp.int32)],
        compiler_params=_cparams(("parallel", "arbitrary")),
        name="hier_moe_grouped",
    )(x2, gffn, wr, br, wgu, wd)


def _pad_lanes(w, width):
    return jnp.pad(w, [(0, 0)] * (w.ndim - 1) + [(0, width - w.shape[-1])])


def _rope_cos_sin(S):
    pos = jnp.arange(S, dtype=F32)
    inv_freq = ROPE_BASE ** (-jnp.arange(0, MLA_ROPE, 2, dtype=F32) / MLA_ROPE)
    ang = pos[:, None] * inv_freq[None, :]
    return jnp.cos(ang), jnp.sin(ang)


def _rope_tables(cos, sin, gq, gk):
    S = cos.shape[0]
    n, half = MLA_NOPE, MLA_ROPE // 2
    z = lambda w: jnp.zeros((S, w), F32)
    gq_ext = jnp.concatenate([gq[:n], gq[n:n + half], gq[n + half:], gq[n + half:], gq[n:n + half]])
    scale = MLA_QK ** -0.5 * LOG2E
    tq = jnp.concatenate([jnp.ones((S, n), F32), cos, cos, -sin, sin], axis=1) * (scale * gq_ext)[None, :]
    gk_rope = jnp.concatenate([jnp.zeros((n,), F32), gk[n:], jnp.zeros((LANES - MLA_QK,), F32)])
    ck = jnp.concatenate([z(n), cos, cos, z(LANES - MLA_QK)], axis=1) * gk_rope[None, :]
    s1k = jnp.concatenate([z(n), -sin, z(LANES - n - half)], axis=1) * jnp.roll(gk_rope, -half)[None, :]
    s2k = jnp.concatenate([z(n + half), sin, z(LANES - MLA_QK)], axis=1) * jnp.roll(gk_rope, half)[None, :]
    return tq, ck, s1k, s2k


def _block_diag(w):
    n, c, d = w.shape
    eye = jnp.eye(n, dtype=w.dtype)
    return (eye[:, None, :, None] * w[:, :, None, :]).reshape(n * c, n * d)


def kernel(x, mem, g_mix, w_in, g_cq, w_uq, g_ckv, w_ukv, g_qn, g_kn, conv_w, conv_b, w_rg, b_rg, w_ig, b_ig,
           lam, g_attn_out, g_rnn_out, w_out, g_xq, g_mem, w_mq, w_mk, w_mv, g_mqn, g_mkn, w_mo, g_ffn,
           w_group, b_group, w_expert, b_expert, w_e_gate, w_e_up, w_e_down):
    B, S, D = x.shape
    H = MLA_HEADS
    tm = min(512, S)
    tk = min(512, S)
    tq = min(1024, S)
    row = lambda a: a.reshape(1, -1)
    cos, sin = _rope_cos_sin(S)
    for l in range(g_mix.shape[0]):
        wi = w_in[l]
        c0, c1, c2 = Q_LORA + KV_LORA, Q_LORA + KV_LORA + MLA_ROPE, Q_LORA + KV_LORA + MLA_ROPE + RNN_WIDTH
        zc = lambda n: jnp.zeros((D, n), wi.dtype)
        win = jnp.concatenate([wi[:, :c0], zc(MLA_NOPE), wi[:, c0:c1], zc(LANES - MLA_QK), wi[:, c1:c2], wi[:, c2:]],
                              axis=1).astype(BF16)
        wq = w_uq[l].reshape(Q_LORA, H, MLA_QK)
        n, half = MLA_NOPE, MLA_ROPE // 2
        wuq = jnp.concatenate([wq, wq[:, :, n + half:], wq[:, :, n:n + half]], axis=-1).reshape(Q_LORA, H * LANES)
        wuq = wuq.astype(BF16)
        wkv = w_ukv[l].reshape(KV_LORA, H, MLA_NOPE + MLA_V)
        wukv = jnp.concatenate([_pad_lanes(wkv[:, :, :MLA_NOPE], LANES).reshape(KV_LORA, H * LANES),
                                _pad_lanes(wkv[:, :, MLA_NOPE:], LANES).reshape(KV_LORA, H * LANES)],
                               axis=1).astype(BF16)
        gkn = _pad_lanes(row(g_kn[l][:MLA_NOPE]), LANES)
        tabs = _rope_tables(cos, sin, g_qn[l], g_kn[l])
        wrg = _block_diag(w_rg[l]).astype(BF16)
        wig = _block_diag(w_ig[l]).astype(BF16)
        gpad = ROUTER_OFF - N_GROUPS
        wr32 = _pad_lanes(jnp.concatenate([_pad_lanes(w_group[l], ROUTER_OFF), w_expert[l]], axis=1), LANES)
        wr_hi = wr32.astype(BF16)
        wr = jnp.concatenate([wr_hi, (wr32 - wr_hi.astype(F32)).astype(BF16)], axis=1)
        br = _pad_lanes(row(jnp.concatenate([b_group[l], jnp.zeros((gpad,), F32), b_expert[l]])), LANES)
        wgu = jnp.concatenate([w_e_gate[l], w_e_up[l]], axis=-1).astype(BF16)
        wd = w_e_down[l].astype(BF16)

        q, k, v, ug, ux = _proj_call(x, row(g_mix[l]), win, row(g_cq[l]), wuq, row(g_ckv[l]), wukv, gkn, tabs, tm)
        o_rnn = _rglru_call(ug, ux, conv_w[l], row(conv_b[l]), wrg, row(b_rg[l]), wig, row(b_ig[l]), row(lam[l]),
                            row(g_rnn_out[l]), tm)
        o_attn = _attn_call(q, k, v, tq, tk, 2)
        km, vm = _memkv_call(mem, row(g_mem[l]), w_mk[l].astype(BF16), w_mv[l].astype(BF16), row(g_mkn[l]))
        x2 = _mix_mem_call(x, o_attn, o_rnn, row(g_attn_out[l]), w_out[l].astype(BF16), row(g_xq[l]),
                           w_mq[l].astype(BF16), row(g_mqn[l]), km, vm, w_mo[l].astype(BF16), tm)
        x = _moe_call(x2.reshape(B * S, D), row(g_ffn[l]), wr, br, wgu, wd, min(1024, B * S), MOE_CHUNK).reshape(B, S, D)
    return x
```

```python
import functools

import jax
import jax.numpy as jnp
from jax import lax
from jax.experimental import pallas as pl
from jax.experimental.pallas import tpu as pltpu

F32 = jnp.float32
BF16 = jnp.bfloat16

EPS = 1e-6
LANES = 128
CHUNK_SHIFT = 6
MLA_HEADS = 8
MLA_NOPE = 64
MLA_ROPE = 32
MLA_QK = MLA_NOPE + MLA_ROPE
MLA_V = 64
Q_LORA = 256
KV_LORA = 128
RNN_WIDTH = 512
RNN_BLOCKS = 8
CONV_WIDTH = 4
LRU_C = 8.0
ROPE_BASE = 10000.0
MEM_HEADS = 4
MEM_HEAD_DIM = 256
N_GROUPS = 4
EXPERTS_PER_GROUP = 8
N_EXPERTS = 32
D_EXPERT = 256
ROUTER_OFF = 8
MOE_CHUNK = 256
NEG = -1e30
LOG2E = 1.4426950408889634
VMEM_LIMIT = 56 * 1024 * 1024


def _rms(x, g):
    return x * lax.rsqrt(jnp.mean(x * x, axis=-1, keepdims=True) + EPS) * g


def _bdot(a, b):
    return jnp.dot(a.astype(BF16), b, preferred_element_type=F32)


def _cparams(sem):
    return pltpu.CompilerParams(dimension_semantics=sem, vmem_limit_bytes=VMEM_LIMIT)


def _const_spec(shape):
    return pl.BlockSpec(shape, lambda *_: (0,) * len(shape))


def _proj_kernel(x_ref, gmix_ref, win_ref, gcq_ref, wuq_ref, gckv_ref, wukv_ref, gkn_ref,
                 tq_ref, ck_ref, s1k_ref, s2k_ref, q_ref, k_ref, v_ref, ug_ref, ux_ref, *, nsplit):
    tm = x_ref.shape[1]
    rs = tm // nsplit
    lane = lax.broadcasted_iota(jnp.int32, (1, LANES), 1)
    ones_col = (lane == MLA_V).astype(F32)
    qmask = (lane < MLA_QK).astype(F32)
    inv = 1.0 / MLA_QK
    half = MLA_ROPE // 2
    gkn = gkn_ref[...]
    for part in range(nsplit):
        rows = slice(part * rs, (part + 1) * rs)
        z = _bdot(_rms(x_ref[0, rows, :], gmix_ref[...]), win_ref[...])
        ug_ref[0, rows, :] = z[:, 512:1024]
        ux_ref[0, rows, :] = z[:, 1024:1536]
        q = _bdot(_rms(z[:, 0:Q_LORA], gcq_ref[...]), wuq_ref[...])
        kv = _bdot(_rms(z[:, Q_LORA:Q_LORA + KV_LORA], gckv_ref[...]), wukv_ref[...])
        kr = z[:, 384:512]
        ss_rope = jnp.sum(kr * kr, axis=-1, keepdims=True)
        rot = (kr * ck_ref[rows, :] + pltpu.roll(kr, LANES - half, axis=1) * s1k_ref[rows, :]
               + pltpu.roll(kr, half, axis=1) * s2k_ref[rows, :])
        rot2 = rot + pltpu.roll(rot, MLA_ROPE, axis=1)
        tq = tq_ref[rows, :]
        for h in range(MLA_HEADS):
            sl = slice(h * LANES, (h + 1) * LANES)
            qh = q[:, sl]
            rq = lax.rsqrt(jnp.sum(qh * qh * qmask, axis=-1, keepdims=True) * inv + EPS)
            q_ref[0, h, rows, :] = (qh * rq * tq).astype(BF16)
            kn = kv[:, sl]
            rk = lax.rsqrt((jnp.sum(kn * kn, axis=-1, keepdims=True) + ss_rope) * inv + EPS)
            k_ref[0, h, rows, :] = ((kn * gkn + rot2) * rk).astype(BF16)
            vsl = slice((MLA_HEADS + h) * LANES, (MLA_HEADS + h + 1) * LANES)
            v_ref[0, h, rows, :] = (kv[:, vsl] + ones_col).astype(BF16)


def _proj_call(x, gmix, win, gcq, wuq, gckv, wukv, gkn, tabs, tm):
    B, S, D = x.shape
    H = MLA_HEADS
    hs = jax.ShapeDtypeStruct((B, H, S, LANES), BF16)
    us = jax.ShapeDtypeStruct((B, S, RNN_WIDTH), F32)
    hspec = pl.BlockSpec((1, H, tm, LANES), lambda b, i: (b, 0, i, 0))
    uspec = pl.BlockSpec((1, tm, RNN_WIDTH), lambda b, i: (b, i, 0))
    tspec = pl.BlockSpec((tm, LANES), lambda b, i: (i, 0))
    return pl.pallas_call(
        functools.partial(_proj_kernel, nsplit=2 if tm % 16 == 0 else 1),
        grid=(B, S // tm),
        in_specs=[pl.BlockSpec((1, tm, D), lambda b, i: (b, i, 0)),
                  _const_spec(gmix.shape), _const_spec(win.shape), _const_spec(gcq.shape),
                  _const_spec(wuq.shape), _const_spec(gckv.shape), _const_spec(wukv.shape),
                  _const_spec(gkn.shape), tspec, tspec, tspec, tspec],
        out_specs=[hspec, hspec, hspec, uspec, uspec],
        out_shape=[hs, hs, hs, us, us],
        compiler_params=_cparams(("parallel", "parallel")),
        name="mla_rglru_in_proj",
    )(x, gmix, win, gcq, wuq, gckv, wukv, gkn, *tabs)


def _rglru_kernel(ug_ref, ux_ref, cw_ref, cb_ref, wrg_ref, brg_ref, wig_ref, big_ref, lam_ref, gout_ref,
                  o_ref, ubuf, a_s, b_s, h_s, hc_s, *, tm):
    W = RNN_WIDTH

    @pl.when(pl.program_id(1) == 0)
    def _():
        ubuf[0:8, :] = jnp.zeros((8, W), F32)
        hc_s[...] = jnp.zeros((8, W), F32)

    ubuf[8:tm + 8, :] = ux_ref[0]
    cw = cw_ref[...]
    xc = cb_ref[...] + sum(ubuf[pl.ds(8 - (CONV_WIDTH - 1) + j, tm), :] * cw[j:j + 1, :] for j in range(CONV_WIDTH))
    ubuf[0:8, :] = ubuf[tm:tm + 8, :]
    sigmoid = lambda y: 0.5 * jnp.tanh(0.5 * y) + 0.5
    r = sigmoid(_bdot(xc, wrg_ref[...]) + brg_ref[...])
    i = sigmoid(_bdot(xc, wig_ref[...]) + big_ref[...])
    nl = -lam_ref[...]
    softplus = jnp.maximum(nl, 0.0) + jnp.log(1.0 + jnp.exp(-jnp.abs(nl)))
    a = jnp.exp((-LRU_C * r) * softplus)
    om = 1.0 - a * a
    bt = jnp.where(om > 0.0, om * lax.rsqrt(om), 0.0) * (i * xc)

    a3 = a.reshape(tm // 8, 8, W)
    b3 = bt.reshape(tm // 8, 8, W)
    row = lax.broadcasted_iota(jnp.int32, (tm // 8, 8, W), 1)
    for d in (1, 2, 4):
        keep = row >= d
        a_sh = jnp.where(keep, pltpu.roll(a3, d, axis=1), 1.0)
        b_sh = jnp.where(keep, pltpu.roll(b3, d, axis=1), 0.0)
        b3 = a3 * b_sh + b3
        a3 = a3 * a_sh
    a_s[...] = a3.reshape(tm, W)
    b_s[...] = b3.reshape(tm, W)

    def body(g, hprev):
        sl = pl.ds(pl.multiple_of(g * 8, 8), 8)
        hr = a_s[sl, :] * hprev + b_s[sl, :]
        h_s[sl, :] = hr
        return jnp.broadcast_to(hr[7:8, :], (8, W))

    hc_s[...] = lax.fori_loop(0, tm // 8, body, hc_s[...])
    o = jax.nn.gelu(ug_ref[0], approximate=True) * h_s[...]
    o_ref[0] = _rms(o, gout_ref[...]).astype(BF16)


def _rglru_call(ug, ux, cw, cb, wrg, brg, wig, big, lam, gout, tm):
    B, S, W = ux.shape
    uspec = pl.BlockSpec((1, tm, W), lambda b, i: (b, i, 0))
    consts = (cw, cb, wrg, brg, wig, big, lam, gout)
    return pl.pallas_call(
        functools.partial(_rglru_kernel, tm=tm),
        grid=(B, S // tm),
        in_specs=[uspec, uspec] + [_const_spec(c.shape) for c in consts],
        out_specs=uspec,
        out_shape=jax.ShapeDtypeStruct((B, S, W), BF16),
        scratch_shapes=[pltpu.VMEM((tm + 8, W), F32), pltpu.VMEM((tm, W), F32), pltpu.VMEM((tm, W), F32),
                        pltpu.VMEM((tm, W), F32), pltpu.VMEM((8, W), F32)],
        compiler_params=_cparams(("parallel", "arbitrary")),
        name="rglru",
    )(ug, ux, *consts)


def _attn_kernel(q_ref, k_ref, v_ref, bias_ref, o_ref, m_s, acc_s, *, tq, tk, hp):
    qi = pl.program_id(2)
    nsub = tq // tk
    m_s[...] = jnp.full(m_s.shape, NEG, F32)
    acc_s[...] = jnp.zeros(acc_s.shape, F32)

    def block(hh, k0, ksz, r0, nrows, masked):
        rows = slice(r0, r0 + nrows)
        s = lax.dot_general(q_ref[0, hh, rows, :], k_ref[0, hh, pl.ds(k0, ksz), :],
                            (((1,), (1,)), ((), ())), preferred_element_type=F32)
        if masked:
            tail = s[:, ksz - tk:] + bias_ref[...]
            s = tail if ksz == tk else jnp.concatenate([s[:, :ksz - tk], tail], axis=1)
        m_prev = m_s[hh, rows, :]
        m_next = jnp.maximum(m_prev, jnp.max(s, axis=1, keepdims=True))
        alpha = jnp.exp2(m_prev - m_next)
        p = jnp.exp2(s - jnp.concatenate([m_next] * (ksz // LANES), axis=1))
        pv = jnp.dot(p.astype(BF16), v_ref[0, hh, pl.ds(k0, ksz), :], preferred_element_type=F32)
        acc_s[hh, rows, :] = acc_s[hh, rows, :] * alpha + pv
        m_s[hh, rows, :] = m_next

    def full_body(kb, carry):
        k0 = pl.multiple_of(kb * tq, tq)
        for hh in range(hp):
            block(hh, k0, tq, 0, tq, False)
        return carry

    lax.fori_loop(0, qi, full_body, 0)
    k0 = pl.multiple_of(qi * tq, tq)
    for j in range(nsub):
        for hh in range(hp):
            block(hh, k0, (j + 1) * tk, j * tk, tk, True)
    lane = lax.broadcasted_iota(jnp.int32, (tq, LANES), 1)
    for pr in range(hp // 2):
        a0, a1 = acc_s[2 * pr], acc_s[2 * pr + 1]
        o0 = a0 / a0[:, MLA_V:MLA_V + 1]
        o1 = a1 / a1[:, MLA_V:MLA_V + 1]
        o_ref[0, :, pr * LANES:(pr + 1) * LANES] = jnp.where(
            lane < MLA_V, o0, pltpu.roll(o1, MLA_V, axis=1)).astype(BF16)


def _attn_call(q, k, v, tq, tk, hp):
    B, H, S, _ = q.shape
    idx = jnp.arange(tk, dtype=jnp.int32) >> CHUNK_SHIFT
    bias = jnp.where(idx[None, :] <= idx[:, None], 0.0, NEG).astype(F32)
    return pl.pallas_call(
        functools.partial(_attn_kernel, tq=tq, tk=tk, hp=hp),
        grid=(B, H // hp, S // tq),
        in_specs=[pl.BlockSpec((1, hp, tq, LANES), lambda b, h, i: (b, h, i, 0)),
                  pl.BlockSpec((1, hp, S, LANES), lambda b, h, i: (b, h, 0, 0)),
                  pl.BlockSpec((1, hp, S, LANES), lambda b, h, i: (b, h, 0, 0)),
                  _const_spec(bias.shape)],
        out_specs=pl.BlockSpec((1, tq, hp * MLA_V), lambda b, h, i: (b, i, h)),
        out_shape=jax.ShapeDtypeStruct((B, S, H * MLA_V), BF16),
        scratch_shapes=[pltpu.VMEM((hp, tq, LANES), F32), pltpu.VMEM((hp, tq, LANES), F32)],
        compiler_params=_cparams(("parallel", "parallel", "arbitrary")),
        name="block_causal_attention",
    )(q, k, v, bias)


def _memkv_kernel(mem_ref, gmem_ref, wmk_ref, wmv_ref, gmkn_ref, k_ref, v_ref):
    m = _rms(mem_ref[0], gmem_ref[...]).astype(BF16)
    k = jnp.dot(m, wmk_ref[...], preferred_element_type=F32)
    v = jnp.dot(m, wmv_ref[...], preferred_element_type=F32)
    g = gmkn_ref[...]
    for h in range(MEM_HEADS):
        sl = slice(h * MEM_HEAD_DIM, (h + 1) * MEM_HEAD_DIM)
        k_ref[0, :, sl] = _rms(k[:, sl], g).astype(BF16)
    v_ref[0] = v.astype(BF16)


def _memkv_call(mem, gmem, wmk, wmv, gmkn):
    B, M, D = mem.shape
    spec = pl.BlockSpec((1, M, D), lambda b: (b, 0, 0))
    os_ = jax.ShapeDtypeStruct((B, M, D), BF16)
    return pl.pallas_call(
        _memkv_kernel,
        grid=(B,),
        in_specs=[spec, _const_spec(gmem.shape), _const_spec(wmk.shape), _const_spec(wmv.shape),
                  _const_spec(gmkn.shape)],
        out_specs=[spec, spec],
        out_shape=[os_, os_],
        compiler_params=_cparams(("parallel",)),
        name="mem_kv",
    )(mem, gmem, wmk, wmv, gmkn)


def _mix_mem_kernel(x_ref, oa_ref, orn_ref, gao_ref, wout_ref, gxq_ref, wmq_ref, gmqn_ref, km_ref, vm_ref,
                    wmo_ref, o_ref):
    oa = _rms(oa_ref[0].astype(F32), gao_ref[...]).astype(BF16)
    mix = jnp.concatenate([oa, orn_ref[0]], axis=-1)
    x1 = x_ref[0] + jnp.dot(mix, wout_ref[...], preferred_element_type=F32)
    qm = _bdot(_rms(x1, gxq_ref[...]), wmq_ref[...])
    g = gmqn_ref[...]
    scale = MEM_HEAD_DIM ** -0.5
    heads = []
    for h in range(MEM_HEADS):
        sl = slice(h * MEM_HEAD_DIM, (h + 1) * MEM_HEAD_DIM)
        qh = (_rms(qm[:, sl], g) * scale).astype(BF16)
        s = lax.dot_general(qh, km_ref[0, :, sl], (((1,), (1,)), ((), ())), preferred_element_type=F32)
        p = jnp.exp(s - jnp.max(s, axis=-1, keepdims=True))
        oh = jnp.dot(p.astype(BF16), vm_ref[0, :, sl], preferred_element_type=F32)
        heads.append((oh / jnp.sum(p, axis=-1, keepdims=True)).astype(BF16))
    o_ref[0] = x1 + jnp.dot(jnp.concatenate(heads, axis=-1), wmo_ref[...], preferred_element_type=F32)


def _mix_mem_call(x, oa, orn, gao, wout, gxq, wmq, gmqn, km, vm, wmo, tm):
    B, S, D = x.shape
    M = km.shape[1]
    xspec = pl.BlockSpec((1, tm, D), lambda b, i: (b, i, 0))
    hspec = pl.BlockSpec((1, tm, D // 2), lambda b, i: (b, i, 0))
    mspec = pl.BlockSpec((1, M, D), lambda b, i: (b, 0, 0))
    return pl.pallas_call(
        _mix_mem_kernel,
        grid=(B, S // tm),
        in_specs=[xspec, hspec, hspec, _const_spec(gao.shape), _const_spec(wout.shape), _const_spec(gxq.shape),
                  _const_spec(wmq.shape), _const_spec(gmqn.shape), mspec, mspec, _const_spec(wmo.shape)],
        out_specs=xspec,
        out_shape=jax.ShapeDtypeStruct((B, S, D), F32),
        compiler_params=_cparams(("parallel", "parallel")),
        name="out_proj_mem_attn",
    )(x, oa, orn, gao, wout, gxq, wmq, gmqn, km, vm, wmo)


def _route_t(lt):
    tm = lt.shape[1]
    srow = lax.broadcasted_iota(jnp.int32, (EXPERTS_PER_GROUP, tm), 0)
    glog = jnp.where(srow < N_GROUPS, lt[0:8], NEG)
    gmax = jnp.max(glog, axis=0, keepdims=True)
    p_g = 1.0 / jnp.sum(jnp.exp(glog - gmax), axis=0, keepdims=True)
    g_idx = jnp.min(jnp.where(glog == gmax, srow, 8), axis=0, keepdims=True)
    el = lt[ROUTER_OFF:ROUTER_OFF + EXPERTS_PER_GROUP]
    for gg in range(1, N_GROUPS):
        lo = ROUTER_OFF + gg * EXPERTS_PER_GROUP
        el = jnp.where(g_idx == gg, lt[lo:lo + EXPERTS_PER_GROUP], el)
    ee = jnp.exp(el - jnp.max(el, axis=0, keepdims=True))
    probs = ee / jnp.sum(ee, axis=0, keepdims=True)
    v1 = jnp.max(probs, axis=0, keepdims=True)
    i1 = jnp.min(jnp.where(probs == v1, srow, 8), axis=0, keepdims=True)
    rest = srow != i1
    v2 = jnp.max(jnp.where(rest, probs, -1.0), axis=0, keepdims=True)
    i2 = jnp.min(jnp.where(rest & (probs == v2), srow, 8), axis=0, keepdims=True)
    tot = v1 + v2
    gl = jnp.where(srow == i1, p_g * (v1 / tot), 0.0) + jnp.where(srow == i2, p_g * (v2 / tot), 0.0)
    return gl, g_idx


def _moe_kernel(x_ref, gffn_ref, wr_ref, br_ref, wgu_ref, wd_ref, o_ref, text_s, metac_s, metar_s, cnt_s, *, ch):
    g = pl.program_id(1)
    tm, D = x_ref.shape

    @pl.when(g == 0)
    def _():
        x = x_ref[...]
        t = _rms(x, gffn_ref[...])
        thi = t.astype(BF16)
        tlo = (t - thi.astype(F32)).astype(BF16)
        l1 = jnp.dot(thi, wr_ref[...], preferred_element_type=F32)
        l2 = jnp.dot(tlo, wr_ref[:, 0:LANES], preferred_element_type=F32)
        logits = l1[:, 0:LANES] + l1[:, LANES:] + l2 + br_ref[...]
        gl, gid = _route_t(logits.T)
        srow = lax.broadcasted_iota(jnp.int32, (8, tm), 0)
        oh = srow == gid
        upper = lax.broadcasted_iota(jnp.int32, (tm, tm), 0) < lax.broadcasted_iota(jnp.int32, (tm, tm), 1)
        pref = jnp.dot(oh.astype(BF16), upper.astype(BF16), preferred_element_type=F32)
        rank = jnp.sum(jnp.where(oh, pref, 0.0), axis=0, keepdims=True)
        rg = jnp.where(srow == 0, rank, jnp.where(srow == 1, gid.astype(F32), 0.0))
        metar_s[...] = rg
        info = jnp.concatenate([gl, rg, jnp.zeros((LANES - 16, tm), F32)], axis=0).T
        metac_s[...] = info
        ghi = info.astype(BF16)
        text_s[:, 0:D] = thi
        text_s[:, D:D + LANES] = ghi
        text_s[:, D + LANES:] = (info - ghi.astype(F32)).astype(BF16)
        for gg in range(N_GROUPS):
            cnt_s[gg] = jnp.sum(oh[gg:gg + 1, :].astype(jnp.int32))
        o_ref[...] = x

    gf = g.astype(F32)
    nch = (cnt_s[g] + (ch - 1)) // ch
    key_row = jnp.where(metar_s[1:2, :] == gf, metar_s[0:1, :], -1.0)
    mc = metac_s[...]
    key_col = jnp.where(mc[:, 9:10] == gf, mc[:, 8:9], -1.0)
    sub = lax.broadcasted_iota(jnp.int32, (ch, tm), 0).astype(F32)
    lan = lax.broadcasted_iota(jnp.int32, (tm, ch), 1).astype(F32)

    def chunk(c, carry):
        base = (c * ch).astype(F32)
        disp = (key_row - base == sub).astype(BF16)
        xc = jnp.dot(disp, text_s[...], preferred_element_type=F32)
        xb = xc[:, :D].astype(BF16)
        gat = xc[:, D:D + LANES] + xc[:, D + LANES:]
        y = jnp.zeros((ch, D), F32)
        for e in range(EXPERTS_PER_GROUP):
            h = jnp.dot(xb, wgu_ref[e], preferred_element_type=F32)
            he = jax.nn.silu(h[:, :D_EXPERT]) * h[:, D_EXPERT:] * gat[:, e:e + 1]
            y = y + _bdot(he, wd_ref[e])
        comb = (key_col - base == lan).astype(BF16)
        o_ref[...] += jnp.dot(comb, y.astype(BF16), preferred_element_type=F32)
        return carry

    lax.fori_loop(0, nch, chunk, 0)


def _moe_call(x2, gffn, wr, br, wgu, wd, tm, ch):
    N, D = x2.shape
    xspec = pl.BlockSpec((tm, D), lambda i, g: (i, 0))
    epg = EXPERTS_PER_GROUP
    return pl.pallas_call(
        functools.partial(_moe_kernel, ch=ch),
        grid=(N // tm, N_GROUPS),
        in_specs=[xspec, _const_spec(gffn.shape), _const_spec(wr.shape), _const_spec(br.shape),
                  pl.BlockSpec((epg, D, 2 * D_EXPERT), lambda i, g: (g, 0, 0)),
                  pl.BlockSpec((epg, D_EXPERT, D), lambda i, g: (g, 0, 0))],
        out_specs=xspec,
        out_shape=jax.ShapeDtypeStruct((N, D), F32),
        scratch_shapes=[pltpu.VMEM((tm, D + 2 * LANES), BF16), pltpu.VMEM((tm, LANES), F32),
                        pltpu.VMEM((8, tm), F32), pltpu.SMEM((N_GROUPS,), jnp.int32)],
        compiler_params=_cparams(("parallel", "arbitrary")),
        name="hier_moe_grouped",
    )(x2, gffn, wr, br, wgu, wd)


def _pad_lanes(w, width):
    return jnp.pad(w, [(0, 0)] * (w.ndim - 1) + [(0, width - w.shape[-1])])


def _rope_cos_sin(S):
    pos = jnp.arange(S, dtype=F32)
    inv_freq = ROPE_BASE ** (-jnp.arange(0, MLA_ROPE, 2, dtype=F32) / MLA_ROPE)
    ang = pos[:, None] * inv_freq[None, :]
    return jnp.cos(ang), jnp.sin(ang)


def _rope_tables(cos, sin, gq, gk):
    S = cos.shape[0]
    n, half = MLA_NOPE, MLA_ROPE // 2
    z = lambda w: jnp.zeros((S, w), F32)
    gq_ext = jnp.concatenate([gq[:n], gq[n:n + half], gq[n + half:], gq[n + half:], gq[n:n + half]])
    scale = MLA_QK ** -0.5 * LOG2E
    tq = jnp.concatenate([jnp.ones((S, n), F32), cos, cos, -sin, sin], axis=1) * (scale * gq_ext)[None, :]
    gk_rope = jnp.concatenate([jnp.zeros((n,), F32), gk[n:], jnp.zeros((LANES - MLA_QK,), F32)])
    ck = jnp.concatenate([z(n), cos, cos, z(LANES - MLA_QK)], axis=1) * gk_rope[None, :]
    s1k = jnp.concatenate([z(n), -sin, z(LANES - n - half)], axis=1) * jnp.roll(gk_rope, -half)[None, :]
    s2k = jnp.concatenate([z(n + half), sin, z(LANES - MLA_QK)], axis=1) * jnp.roll(gk_rope, half)[None, :]
    return tq, ck, s1k, s2k


def _block_diag(w):
    n, c, d = w.shape
    eye = jnp.eye(n, dtype=w.dtype)
    return (eye[:, None, :, None] * w[:, :, None, :]).reshape(n * c, n * d)


def kernel(x, mem, g_mix, w_in, g_cq, w_uq, g_ckv, w_ukv, g_qn, g_kn, conv_w, conv_b, w_rg, b_rg, w_ig, b_ig,
           lam, g_attn_out, g_rnn_out, w_out, g_xq, g_mem, w_mq, w_mk, w_mv, g_mqn, g_mkn, w_mo, g_ffn,
           w_group, b_group, w_expert, b_expert, w_e_gate, w_e_up, w_e_down):
    B, S, D = x.shape
    H = MLA_HEADS
    tm = min(1024, S)
    tk = min(512, S)
    tq = min(1024, S)
    row = lambda a: a.reshape(1, -1)
    cos, sin = _rope_cos_sin(S)
    for l in range(g_mix.shape[0]):
        wi = w_in[l]
        c0, c1, c2 = Q_LORA + KV_LORA, Q_LORA + KV_LORA + MLA_ROPE, Q_LORA + KV_LORA + MLA_ROPE + RNN_WIDTH
        zc = lambda n: jnp.zeros((D, n), wi.dtype)
        win = jnp.concatenate([wi[:, :c0], zc(MLA_NOPE), wi[:, c0:c1], zc(LANES - MLA_QK), wi[:, c1:c2], wi[:, c2:]],
                              axis=1).astype(BF16)
        wq = w_uq[l].reshape(Q_LORA, H, MLA_QK)
        n, half = MLA_NOPE, MLA_ROPE // 2
        wuq = jnp.concatenate([wq, wq[:, :, n + half:], wq[:, :, n:n + half]], axis=-1).reshape(Q_LORA, H * LANES)
        wuq = wuq.astype(BF16)
        wkv = w_ukv[l].reshape(KV_LORA, H, MLA_NOPE + MLA_V)
        wukv = jnp.concatenate([_pad_lanes(wkv[:, :, :MLA_NOPE], LANES).reshape(KV_LORA, H * LANES),
                                _pad_lanes(wkv[:, :, MLA_NOPE:], LANES).reshape(KV_LORA, H * LANES)],
                               axis=1).astype(BF16)
        gkn = _pad_lanes(row(g_kn[l][:MLA_NOPE]), LANES)
        tabs = _rope_tables(cos, sin, g_qn[l], g_kn[l])
        wrg = _block_diag(w_rg[l]).astype(BF16)
        wig = _block_diag(w_ig[l]).astype(BF16)
        gpad = ROUTER_OFF - N_GROUPS
        wr32 = _pad_lanes(jnp.concatenate([_pad_lanes(w_group[l], ROUTER_OFF), w_expert[l]], axis=1), LANES)
        wr_hi = wr32.astype(BF16)
        wr = jnp.concatenate([wr_hi, (wr32 - wr_hi.astype(F32)).astype(BF16)], axis=1)
        br = _pad_lanes(row(jnp.concatenate([b_group[l], jnp.zeros((gpad,), F32), b_expert[l]])), LANES)
        wgu = jnp.concatenate([w_e_gate[l], w_e_up[l]], axis=-1).astype(BF16)
        wd = w_e_down[l].astype(BF16)

        q, k, v, ug, ux = _proj_call(x, row(g_mix[l]), win, row(g_cq[l]), wuq, row(g_ckv[l]), wukv, gkn, tabs, tm)
        o_rnn = _rglru_call(ug, ux, conv_w[l], row(conv_b[l]), wrg, row(b_rg[l]), wig, row(b_ig[l]), row(lam[l]),
                            row(g_rnn_out[l]), tm)
        o_attn = _attn_call(q, k, v, tq, tk, 4)
        km, vm = _memkv_call(mem, row(g_mem[l]), w_mk[l].astype(BF16), w_mv[l].astype(BF16), row(g_mkn[l]))
        x2 = _mix_mem_call(x, o_attn, o_rnn, row(g_attn_out[l]), w_out[l].astype(BF16), row(g_xq[l]),
                           w_mq[l].astype(BF16), row(g_mqn[l]), km, vm, w_mo[l].astype(BF16), tm)
        x = _moe_call(x2.reshape(B * S, D), row(g_ffn[l]), wr, br, wgu, wd, min(1024, B * S), MOE_CHUNK).reshape(B, S, D)
    return x
```

```python
import functools

import jax
import jax.numpy as jnp
from jax import lax
from jax.experimental import pallas as pl
from jax.experimental.pallas import tpu as pltpu

F32 = jnp.float32
BF16 = jnp.bfloat16

EPS = 1e-6
LANES = 128
CHUNK_SHIFT = 6
MLA_HEADS = 8
MLA_NOPE = 64
MLA_ROPE = 32
MLA_QK = MLA_NOPE + MLA_ROPE
MLA_V = 64
Q_LORA = 256
KV_LORA = 128
RNN_WIDTH = 512
RNN_BLOCKS = 8
CONV_WIDTH = 4
LRU_C = 8.0
ROPE_BASE = 10000.0
MEM_HEADS = 4
MEM_HEAD_DIM = 256
N_GROUPS = 4
EXPERTS_PER_GROUP = 8
N_EXPERTS = 32
D_EXPERT = 256
ROUTER_OFF = 8
MOE_CHUNK = 256
NEG = -1e30
LOG2E = 1.4426950408889634
VMEM_LIMIT = 56 * 1024 * 1024


def _rms(x, g):
    return x * lax.rsqrt(jnp.mean(x * x, axis=-1, keepdims=True) + EPS) * g


def _bdot(a, b):
    return jnp.dot(a.astype(BF16), b, preferred_element_type=F32)


def _cparams(sem):
    return pltpu.CompilerParams(dimension_semantics=sem, vmem_limit_bytes=VMEM_LIMIT)


def _const_spec(shape):
    return pl.BlockSpec(shape, lambda *_: (0,) * len(shape))


def _proj_kernel(x_ref, gmix_ref, win_ref, gcq_ref, wuq_ref, gckv_ref, wukv_ref, gkn_ref,
                 tq_ref, ck_ref, s1k_ref, s2k_ref, q_ref, k_ref, v_ref, ug_ref, ux_ref, *, nsplit):
    tm = x_ref.shape[1]
    rs = tm // nsplit
    G = tm // 8
    lane = lax.broadcasted_iota(jnp.int32, (1, LANES), 1)
    ones_col = (lane == MLA_V).astype(F32)
    qmask = (lane < MLA_QK).astype(F32)
    inv = 1.0 / MLA_QK
    half = MLA_ROPE // 2
    gkn = gkn_ref[...]
    for part in range(nsplit):
        rows = slice(part * rs, (part + 1) * rs)
        z = _bdot(_rms(x_ref[0, rows, :], gmix_ref[...]), win_ref[...])
        for sg in range(rs // G):
            seg_rows = pl.ds(part * (rs // G) + sg, G, stride=8)
            zs = z[sg * G:(sg + 1) * G]
            gate = jax.nn.gelu(zs[:, 512:1024], approximate=True)
            for c in range(RNN_WIDTH // LANES):
                ug_ref[0, c, seg_rows, :] = gate[:, c * LANES:(c + 1) * LANES]
                ux_ref[0, c, seg_rows, :] = zs[:, 1024 + c * LANES:1024 + (c + 1) * LANES]
        q = _bdot(_rms(z[:, 0:Q_LORA], gcq_ref[...]), wuq_ref[...])
        kv = _bdot(_rms(z[:, Q_LORA:Q_LORA + KV_LORA], gckv_ref[...]), wukv_ref[...])
        kr = z[:, 384:512]
        ss_rope = jnp.sum(kr * kr, axis=-1, keepdims=True)
        rot = (kr * ck_ref[rows, :] + pltpu.roll(kr, LANES - half, axis=1) * s1k_ref[rows, :]
               + pltpu.roll(kr, half, axis=1) * s2k_ref[rows, :])
        rot2 = rot + pltpu.roll(rot, MLA_ROPE, axis=1)
        tq = tq_ref[rows, :]
        for h in range(MLA_HEADS):
            sl = slice(h * LANES, (h + 1) * LANES)
            qh = q[:, sl]
            rq = lax.rsqrt(jnp.sum(qh * qh * qmask, axis=-1, keepdims=True) * inv + EPS)
            q_ref[0, h, rows, :] = (qh * rq * tq).astype(BF16)
            kn = kv[:, sl]
            rk = lax.rsqrt((jnp.sum(kn * kn, axis=-1, keepdims=True) + ss_rope) * inv + EPS)
            k_ref[0, h, rows, :] = ((kn * gkn + rot2) * rk).astype(BF16)
            vsl = slice((MLA_HEADS + h) * LANES, (MLA_HEADS + h + 1) * LANES)
            v_ref[0, h, rows, :] = (kv[:, vsl] + ones_col).astype(BF16)


def _proj_call(x, gmix, win, gcq, wuq, gckv, wukv, gkn, tabs, tm):
    B, S, D = x.shape
    H = MLA_HEADS
    hs = jax.ShapeDtypeStruct((B, H, S, LANES), BF16)
    us = jax.ShapeDtypeStruct((B, RNN_WIDTH // LANES, S, LANES), F32)
    hspec = pl.BlockSpec((1, H, tm, LANES), lambda b, i: (b, 0, i, 0))
    uspec = pl.BlockSpec((1, RNN_WIDTH // LANES, tm, LANES), lambda b, i: (b, 0, i, 0))
    tspec = pl.BlockSpec((tm, LANES), lambda b, i: (i, 0))
    return pl.pallas_call(
        functools.partial(_proj_kernel, nsplit=2 if tm % 16 == 0 else 1),
        grid=(B, S // tm),
        in_specs=[pl.BlockSpec((1, tm, D), lambda b, i: (b, i, 0)),
                  _const_spec(gmix.shape), _const_spec(win.shape), _const_spec(gcq.shape),
                  _const_spec(wuq.shape), _const_spec(gckv.shape), _const_spec(wukv.shape),
                  _const_spec(gkn.shape), tspec, tspec, tspec, tspec],
        out_specs=[hspec, hspec, hspec, uspec, uspec],
        out_shape=[hs, hs, hs, us, us],
        compiler_params=_cparams(("parallel", "parallel")),
        name="mla_rglru_in_proj",
    )(x, gmix, win, gcq, wuq, gckv, wukv, gkn, *tabs)


def _rglru_kernel(ug_ref, ux_ref, cw_ref, cb_ref, wrg_ref, brg_ref, wig_ref, big_ref, lam_ref, gout_ref,
                  o_ref, halo_s, hc_s, o_s, *, tm):
    W = RNN_WIDTH
    G = tm // 8
    nh = CONV_WIDTH - 1

    @pl.when(pl.program_id(1) == 0)
    def _():
        halo_s[...] = jnp.zeros((nh, 8, W), F32)
        hc_s[...] = jnp.zeros((8, W), F32)

    nlt = W // LANES
    u3 = jnp.concatenate([ux_ref[0, c] for c in range(nlt)], axis=-1).reshape(G, 8, W)
    row3 = lax.broadcasted_iota(jnp.int32, (nh, 8, W), 1)
    head = pltpu.roll(jnp.where(row3 == 7, halo_s[...], u3[G - nh:]), 1, axis=1)
    halo_s[...] = u3[G - nh:]
    ext = jnp.concatenate([head, u3], axis=0)
    cw = cw_ref[...]
    xc3 = cb_ref[...] + sum(ext[j:j + G] * cw[j:j + 1, :] for j in range(CONV_WIDTH))
    xc = xc3.reshape(tm, W)
    sigmoid = lambda y: 0.5 * jnp.tanh(0.5 * y) + 0.5
    r = sigmoid(_bdot(xc, wrg_ref[...]) + brg_ref[...])
    i = sigmoid(_bdot(xc, wig_ref[...]) + big_ref[...])
    nl = -lam_ref[...]
    softplus = jnp.maximum(nl, 0.0) + jnp.log(1.0 + jnp.exp(-jnp.abs(nl)))
    a = jnp.exp((-LRU_C * r) * softplus)
    om = 1.0 - a * a
    bt = jnp.where(om > 0.0, om * lax.rsqrt(om), 0.0) * (i * xc)
    a3 = a.reshape(G, 8, W)
    b3 = bt.reshape(G, 8, W)

    h = jnp.zeros((8, W), F32)
    p = jnp.ones((8, W), F32)
    hs, ps = [], []
    for g in range(G):
        h = a3[g] * h + b3[g]
        p = p * a3[g]
        hs.append(h)
        ps.append(p)
    rowi = lax.broadcasted_iota(jnp.int32, (8, W), 0)
    s0 = jnp.where(rowi == 0, hc_s[...], 0.0)
    for rr in range(1, 8):
        s0 = jnp.where(rowi == rr, pltpu.roll(h + p * s0, 1, axis=0), s0)
    hc_s[...] = jnp.broadcast_to((h + p * s0)[7:8, :], (8, W))
    h3 = jnp.stack(hs, axis=0) + jnp.stack(ps, axis=0) * s0[None]
    o = jnp.concatenate([ug_ref[0, c] for c in range(nlt)], axis=-1) * h3.reshape(tm, W)
    on = _rms(o, gout_ref[...])
    for c in range(nlt):
        o_s[c] = on[:, c * LANES:(c + 1) * LANES]
    for seg in range(8):
        o_ref[0, seg * G:(seg + 1) * G, :] = jnp.concatenate(
            [o_s[c, pl.ds(seg, G, stride=8), :] for c in range(nlt)], axis=-1).astype(BF16)


def _rglru_call(ug, ux, cw, cb, wrg, brg, wig, big, lam, gout, tm):
    B, nlt, S, _ = ux.shape
    W = nlt * LANES
    uspec = pl.BlockSpec((1, nlt, tm, LANES), lambda b, i: (b, 0, i, 0))
    consts = (cw, cb, wrg, brg, wig, big, lam, gout)
    return pl.pallas_call(
        functools.partial(_rglru_kernel, tm=tm),
        grid=(B, S // tm),
        in_specs=[uspec, uspec] + [_const_spec(c.shape) for c in consts],
        out_specs=pl.BlockSpec((1, tm, W), lambda b, i: (b, i, 0)),
        out_shape=jax.ShapeDtypeStruct((B, S, W), BF16),
        scratch_shapes=[pltpu.VMEM((CONV_WIDTH - 1, 8, W), F32), pltpu.VMEM((8, W), F32),
                        pltpu.VMEM((W // LANES, tm, LANES), F32)],
        compiler_params=_cparams(("parallel", "arbitrary")),
        name="rglru",
    )(ug, ux, *consts)


def _attn_kernel(q_ref, k_ref, v_ref, bias_ref, o_ref, m_s, acc_s, *, tq, tk, hp):
    qi = pl.program_id(2)
    nsub = tq // tk
    m_s[...] = jnp.full(m_s.shape, NEG, F32)
    acc_s[...] = jnp.zeros(acc_s.shape, F32)

    def block(hh, k0, ksz, r0, nrows, masked):
        rows = slice(r0, r0 + nrows)
        s = lax.dot_general(q_ref[0, hh, rows, :], k_ref[0, hh, pl.ds(k0, ksz), :],
                            (((1,), (1,)), ((), ())), preferred_element_type=F32)
        if masked:
            tail = s[:, ksz - tk:] + bias_ref[...]
            s = tail if ksz == tk else jnp.concatenate([s[:, :ksz - tk], tail], axis=1)
        m_prev = m_s[hh, rows, :]
        m_next = jnp.maximum(m_prev, jnp.max(s, axis=1, keepdims=True))
        alpha = jnp.exp2(m_prev - m_next)
        p = jnp.exp2(s - jnp.concatenate([m_next] * (ksz // LANES), axis=1))
        pv = jnp.dot(p.astype(BF16), v_ref[0, hh, pl.ds(k0, ksz), :], preferred_element_type=F32)
        acc_s[hh, rows, :] = acc_s[hh, rows, :] * alpha + pv
        m_s[hh, rows, :] = m_next

    def full_body(kb, carry):
        k0 = pl.multiple_of(kb * tq, tq)
        for hh in range(hp):
            block(hh, k0, tq, 0, tq, False)
        return carry

    lax.fori_loop(0, qi, full_body, 0)
    k0 = pl.multiple_of(qi * tq, tq)
    for j in range(nsub):
        for hh in range(hp):
            block(hh, k0, (j + 1) * tk, j * tk, tk, True)
    lane = lax.broadcasted_iota(jnp.int32, (tq, LANES), 1)
    for pr in range(hp // 2):
        a0, a1 = acc_s[2 * pr], acc_s[2 * pr + 1]
        o0 = a0 / a0[:, MLA_V:MLA_V + 1]
        o1 = a1 / a1[:, MLA_V:MLA_V + 1]
        o_ref[0, :, pr * LANES:(pr + 1) * LANES] = jnp.where(
            lane < MLA_V, o0, pltpu.roll(o1, MLA_V, axis=1)).astype(BF16)


def _attn_call(q, k, v, tq, tk, hp):
    B, H, S, _ = q.shape
    idx = jnp.arange(tk, dtype=jnp.int32) >> CHUNK_SHIFT
    bias = jnp.where(idx[None, :] <= idx[:, None], 0.0, NEG).astype(F32)
    return pl.pallas_call(
        functools.partial(_attn_kernel, tq=tq, tk=tk, hp=hp),
        grid=(B, H // hp, S // tq),
        in_specs=[pl.BlockSpec((1, hp, tq, LANES), lambda b, h, i: (b, h, i, 0)),
                  pl.BlockSpec((1, hp, S, LANES), lambda b, h, i: (b, h, 0, 0)),
                  pl.BlockSpec((1, hp, S, LANES), lambda b, h, i: (b, h, 0, 0)),
                  _const_spec(bias.shape)],
        out_specs=pl.BlockSpec((1, tq, hp * MLA_V), lambda b, h, i: (b, i, h)),
        out_shape=jax.ShapeDtypeStruct((B, S, H * MLA_V), BF16),
        scratch_shapes=[pltpu.VMEM((hp, tq, LANES), F32), pltpu.VMEM((hp, tq, LANES), F32)],
        compiler_params=_cparams(("parallel", "parallel", "arbitrary")),
        name="block_causal_attention",
    )(q, k, v, bias)


def _memkv_kernel(mem_ref, gmem_ref, wmk_ref, wmv_ref, gmkn_ref, k_ref, v_ref):
    m = _rms(mem_ref[0], gmem_ref[...]).astype(BF16)
    k = jnp.dot(m, wmk_ref[...], preferred_element_type=F32)
    v = jnp.dot(m, wmv_ref[...], preferred_element_type=F32)
    g = gmkn_ref[...]
    for h in range(MEM_HEADS):
        sl = slice(h * MEM_HEAD_DIM, (h + 1) * MEM_HEAD_DIM)
        k_ref[0, :, sl] = _rms(k[:, sl], g).astype(BF16)
    v_ref[0] = v.astype(BF16)


def _memkv_call(mem, gmem, wmk, wmv, gmkn):
    B, M, D = mem.shape
    spec = pl.BlockSpec((1, M, D), lambda b: (b, 0, 0))
    os_ = jax.ShapeDtypeStruct((B, M, D), BF16)
    return pl.pallas_call(
        _memkv_kernel,
        grid=(B,),
        in_specs=[spec, _const_spec(gmem.shape), _const_spec(wmk.shape), _const_spec(wmv.shape),
                  _const_spec(gmkn.shape)],
        out_specs=[spec, spec],
        out_shape=[os_, os_],
        compiler_params=_cparams(("parallel",)),
        name="mem_kv",
    )(mem, gmem, wmk, wmv, gmkn)


def _mix_mem_kernel(x_ref, oa_ref, orn_ref, gao_ref, wout_ref, gxq_ref, wmq_ref, gmqn_ref, km_ref, vm_ref,
                    wmo_ref, o_ref):
    oa = _rms(oa_ref[0].astype(F32), gao_ref[...]).astype(BF16)
    mix = jnp.concatenate([oa, orn_ref[0]], axis=-1)
    x1 = x_ref[0] + jnp.dot(mix, wout_ref[...], preferred_element_type=F32)
    qm = _bdot(_rms(x1, gxq_ref[...]), wmq_ref[...])
    g = gmqn_ref[...]
    scale = MEM_HEAD_DIM ** -0.5
    heads = []
    for h in range(MEM_HEADS):
        sl = slice(h * MEM_HEAD_DIM, (h + 1) * MEM_HEAD_DIM)
        qh = (_rms(qm[:, sl], g) * scale).astype(BF16)
        s = lax.dot_general(qh, km_ref[0, :, sl], (((1,), (1,)), ((), ())), preferred_element_type=F32)
        p = jnp.exp(s - jnp.max(s, axis=-1, keepdims=True))
        oh = jnp.dot(p.astype(BF16), vm_ref[0, :, sl], preferred_element_type=F32)
        heads.append((oh / jnp.sum(p, axis=-1, keepdims=True)).astype(BF16))
    o_ref[0] = x1 + jnp.dot(jnp.concatenate(heads, axis=-1), wmo_ref[...], preferred_element_type=F32)


def _mix_mem_call(x, oa, orn, gao, wout, gxq, wmq, gmqn, km, vm, wmo, tm):
    B, S, D = x.shape
    M = km.shape[1]
    xspec = pl.BlockSpec((1, tm, D), lambda b, i: (b, i, 0))
    hspec = pl.BlockSpec((1, tm, D // 2), lambda b, i: (b, i, 0))
    mspec = pl.BlockSpec((1, M, D), lambda b, i: (b, 0, 0))
    return pl.pallas_call(
        _mix_mem_kernel,
        grid=(B, S // tm),
        in_specs=[xspec, hspec, hspec, _const_spec(gao.shape), _const_spec(wout.shape), _const_spec(gxq.shape),
                  _const_spec(wmq.shape), _const_spec(gmqn.shape), mspec, mspec, _const_spec(wmo.shape)],
        out_specs=xspec,
        out_shape=jax.ShapeDtypeStruct((B, S, D), F32),
        compiler_params=_cparams(("parallel", "parallel")),
        name="out_proj_mem_attn",
    )(x, oa, orn, gao, wout, gxq, wmq, gmqn, km, vm, wmo)


def _route_t(lt):
    tm = lt.shape[1]
    srow = lax.broadcasted_iota(jnp.int32, (EXPERTS_PER_GROUP, tm), 0)
    glog = jnp.where(srow < N_GROUPS, lt[0:8], NEG)
    gmax = jnp.max(glog, axis=0, keepdims=True)
    p_g = 1.0 / jnp.sum(jnp.exp(glog - gmax), axis=0, keepdims=True)
    g_idx = jnp.min(jnp.where(glog == gmax, srow, 8), axis=0, keepdims=True)
    el = lt[ROUTER_OFF:ROUTER_OFF + EXPERTS_PER_GROUP]
    for gg in range(1, N_GROUPS):
        lo = ROUTER_OFF + gg * EXPERTS_PER_GROUP
        el = jnp.where(g_idx == gg, lt[lo:lo + EXPERTS_PER_GROUP], el)
    ee = jnp.exp(el - jnp.max(el, axis=0, keepdims=True))
    probs = ee / jnp.sum(ee, axis=0, keepdims=True)
    v1 = jnp.max(probs, axis=0, keepdims=True)
    i1 = jnp.min(jnp.where(probs == v1, srow, 8), axis=0, keepdims=True)
    rest = srow != i1
    v2 = jnp.max(jnp.where(rest, probs, -1.0), axis=0, keepdims=True)
    i2 = jnp.min(jnp.where(rest & (probs == v2), srow, 8), axis=0, keepdims=True)
    tot = v1 + v2
    gl = jnp.where(srow == i1, p_g * (v1 / tot), 0.0) + jnp.where(srow == i2, p_g * (v2 / tot), 0.0)
    return gl, g_idx


def _moe_kernel(x_ref, gffn_ref, wr_ref, br_ref, wgu_ref, wd_ref, o_ref, text_s, metac_s, metar_s, cnt_s, *, ch):
    g = pl.program_id(1)
    tm, D = x_ref.shape

    @pl.when(g == 0)
    def _():
        x = x_ref[...]
        t = _rms(x, gffn_ref[...])
        thi = t.astype(BF16)
        tlo = (t - thi.astype(F32)).astype(BF16)
        l1 = jnp.dot(thi, wr_ref[...], preferred_element_type=F32)
        l2 = jnp.dot(tlo, wr_ref[:, 0:LANES], preferred_element_type=F32)
        logits = l1[:, 0:LANES] + l1[:, LANES:] + l2 + br_ref[...]
        gl, gid = _route_t(logits.T)
        srow = lax.broadcasted_iota(jnp.int32, (8, tm), 0)
        oh = srow == gid
        upper = lax.broadcasted_iota(jnp.int32, (tm, tm), 0) < lax.broadcasted_iota(jnp.int32, (tm, tm), 1)
        pref = jnp.dot(oh.astype(BF16), upper.astype(BF16), preferred_element_type=F32)
        rank = jnp.sum(jnp.where(oh, pref, 0.0), axis=0, keepdims=True)
        rg = jnp.where(srow == 0, rank, jnp.where(srow == 1, gid.astype(F32), 0.0))
        metar_s[...] = rg
        info = jnp.concatenate([gl, rg, jnp.zeros((LANES - 16, tm), F32)], axis=0).T
        metac_s[...] = info
        ghi = info.astype(BF16)
        text_s[:, 0:D] = thi
        text_s[:, D:D + LANES] = ghi
        text_s[:, D + LANES:] = (info - ghi.astype(F32)).astype(BF16)
        for gg in range(N_GROUPS):
            cnt_s[gg] = jnp.sum(oh[gg:gg + 1, :].astype(jnp.int32))
        o_ref[...] = x

    gf = g.astype(F32)
    nch = (cnt_s[g] + (ch - 1)) // ch
    key_row = jnp.where(metar_s[1:2, :] == gf, metar_s[0:1, :], -1.0).astype(jnp.int32)
    mc = metac_s[...]
    key_col = jnp.where(mc[:, 9:10] == gf, mc[:, 8:9], -1.0).astype(jnp.int32)
    sub = lax.broadcasted_iota(jnp.int32, (ch, tm), 0)
    lan = lax.broadcasted_iota(jnp.int32, (tm, ch), 1)

    def chunk(c, carry):
        base = c * ch
        disp = (key_row - base == sub).astype(BF16)
        xc = jnp.dot(disp, text_s[...], preferred_element_type=F32)
        xb = xc[:, :D].astype(BF16)
        gat = xc[:, D:D + LANES] + xc[:, D + LANES:]
        y = jnp.zeros((ch, D), F32)
        for e in range(EXPERTS_PER_GROUP):
            h = jnp.dot(xb, wgu_ref[e], preferred_element_type=F32)
            he = jax.nn.silu(h[:, :D_EXPERT]) * h[:, D_EXPERT:] * gat[:, e:e + 1]
            y = y + _bdot(he, wd_ref[e])
        comb = (key_col - base == lan).astype(BF16)
        o_ref[...] += jnp.dot(comb, y.astype(BF16), preferred_element_type=F32)
        return carry

    lax.fori_loop(0, nch, chunk, 0)


def _moe_call(x2, gffn, wr, br, wgu, wd, tm, ch):
    N, D = x2.shape
    xspec = pl.BlockSpec((tm, D), lambda i, g: (i, 0))
    epg = EXPERTS_PER_GROUP
    return pl.pallas_call(
        functools.partial(_moe_kernel, ch=ch),
        grid=(N // tm, N_GROUPS),
        in_specs=[xspec, _const_spec(gffn.shape), _const_spec(wr.shape), _const_spec(br.shape),
                  pl.BlockSpec((epg, D, 2 * D_EXPERT), lambda i, g: (g, 0, 0)),
                  pl.BlockSpec((epg, D_EXPERT, D), lambda i, g: (g, 0, 0))],
        out_specs=xspec,
        out_shape=jax.ShapeDtypeStruct((N, D), F32),
        scratch_shapes=[pltpu.VMEM((tm, D + 2 * LANES), BF16), pltpu.VMEM((tm, LANES), F32),
                        pltpu.VMEM((8, tm), F32), pltpu.SMEM((N_GROUPS,), jnp.int32)],
        compiler_params=_cparams(("parallel", "arbitrary")),
        name="hier_moe_grouped",
    )(x2, gffn, wr, br, wgu, wd)


def _pad_lanes(w, width):
    return jnp.pad(w, [(0, 0)] * (w.ndim - 1) + [(0, width - w.shape[-1])])


def _rope_cos_sin(S):
    pos = jnp.arange(S, dtype=F32)
    inv_freq = ROPE_BASE ** (-jnp.arange(0, MLA_ROPE, 2, dtype=F32) / MLA_ROPE)
    ang = pos[:, None] * inv_freq[None, :]
    return jnp.cos(ang), jnp.sin(ang)


def _rope_tables(cos, sin, gq, gk):
    S = cos.shape[0]
    n, half = MLA_NOPE, MLA_ROPE // 2
    z = lambda w: jnp.zeros((S, w), F32)
    gq_ext = jnp.concatenate([gq[:n], gq[n:n + half], gq[n + half:], gq[n + half:], gq[n:n + half]])
    scale = MLA_QK ** -0.5 * LOG2E
    tq = jnp.concatenate([jnp.ones((S, n), F32), cos, cos, -sin, sin], axis=1) * (scale * gq_ext)[None, :]
    gk_rope = jnp.concatenate([jnp.zeros((n,), F32), gk[n:], jnp.zeros((LANES - MLA_QK,), F32)])
    ck = jnp.concatenate([z(n), cos, cos, z(LANES - MLA_QK)], axis=1) * gk_rope[None, :]
    s1k = jnp.concatenate([z(n), -sin, z(LANES - n - half)], axis=1) * jnp.roll(gk_rope, -half)[None, :]
    s2k = jnp.concatenate([z(n + half), sin, z(LANES - MLA_QK)], axis=1) * jnp.roll(gk_rope, half)[None, :]
    return tq, ck, s1k, s2k


def _block_diag(w):
    n, c, d = w.shape
    eye = jnp.eye(n, dtype=w.dtype)
    return (eye[:, None, :, None] * w[:, :, None, :]).reshape(n * c, n * d)


def kernel(x, mem, g_mix, w_in, g_cq, w_uq, g_ckv, w_ukv, g_qn, g_kn, conv_w, conv_b, w_rg, b_rg, w_ig, b_ig,
           lam, g_attn_out, g_rnn_out, w_out, g_xq, g_mem, w_mq, w_mk, w_mv, g_mqn, g_mkn, w_mo, g_ffn,
           w_group, b_group, w_expert, b_expert, w_e_gate, w_e_up, w_e_down):
    B, S, D = x.shape
    H = MLA_HEADS
    tm = min(1024, S)
    tk = min(512, S)
    tq = min(1024, S)
    row = lambda a: a.reshape(1, -1)
    cos, sin = _rope_cos_sin(S)
    for l in range(g_mix.shape[0]):
        wi = w_in[l]
        c0, c1, c2 = Q_LORA + KV_LORA, Q_LORA + KV_LORA + MLA_ROPE, Q_LORA + KV_LORA + MLA_ROPE + RNN_WIDTH
        zc = lambda n: jnp.zeros((D, n), wi.dtype)
        win = jnp.concatenate([wi[:, :c0], zc(MLA_NOPE), wi[:, c0:c1], zc(LANES - MLA_QK), wi[:, c1:c2], wi[:, c2:]],
                              axis=1).astype(BF16)
        wq = w_uq[l].reshape(Q_LORA, H, MLA_QK)
        n, half = MLA_NOPE, MLA_ROPE // 2
        wuq = jnp.concatenate([wq, wq[:, :, n + half:], wq[:, :, n:n + half]], axis=-1).reshape(Q_LORA, H * LANES)
        wuq = wuq.astype(BF16)
        wkv = w_ukv[l].reshape(KV_LORA, H, MLA_NOPE + MLA_V)
        wukv = jnp.concatenate([_pad_lanes(wkv[:, :, :MLA_NOPE], LANES).reshape(KV_LORA, H * LANES),
                                _pad_lanes(wkv[:, :, MLA_NOPE:], LANES).reshape(KV_LORA, H * LANES)],
                               axis=1).astype(BF16)
        gkn = _pad_lanes(row(g_kn[l][:MLA_NOPE]), LANES)
        tabs = _rope_tables(cos, sin, g_qn[l], g_kn[l])
        wrg = _block_diag(w_rg[l]).astype(BF16)
        wig = _block_diag(w_ig[l]).astype(BF16)
        gpad = ROUTER_OFF - N_GROUPS
        wr32 = _pad_lanes(jnp.concatenate([_pad_lanes(w_group[l], ROUTER_OFF), w_expert[l]], axis=1), LANES)
        wr_hi = wr32.astype(BF16)
        wr = jnp.concatenate([wr_hi, (wr32 - wr_hi.astype(F32)).astype(BF16)], axis=1)
        br = _pad_lanes(row(jnp.concatenate([b_group[l], jnp.zeros((gpad,), F32), b_expert[l]])), LANES)
        wgu = jnp.concatenate([w_e_gate[l], w_e_up[l]], axis=-1).astype(BF16)
        wd = w_e_down[l].astype(BF16)

        q, k, v, ug, ux = _proj_call(x, row(g_mix[l]), win, row(g_cq[l]), wuq, row(g_ckv[l]), wukv, gkn, tabs, tm)
        o_rnn = _rglru_call(ug, ux, conv_w[l], row(conv_b[l]), wrg, row(b_rg[l]), wig, row(b_ig[l]), row(lam[l]),
                            row(g_rnn_out[l]), tm)
        o_attn = _attn_call(q, k, v, tq, tk, 4)
        km, vm = _memkv_call(mem, row(g_mem[l]), w_mk[l].astype(BF16), w_mv[l].astype(BF16), row(g_mkn[l]))
        x2 = _mix_mem_call(x, o_attn, o_rnn, row(g_attn_out[l]), w_out[l].astype(BF16), row(g_xq[l]),
                           w_mq[l].astype(BF16), row(g_mqn[l]), km, vm, w_mo[l].astype(BF16), tm)
        x = _moe_call(x2.reshape(B * S, D), row(g_ffn[l]), wr, br, wgu, wd, min(1024, B * S), MOE_CHUNK).reshape(B, S, D)
    return x
```

```python
import functools

import jax
import jax.numpy as jnp
from jax import lax
from jax.experimental import pallas as pl
from jax.experimental.pallas import tpu as pltpu

F32 = jnp.float32
BF16 = jnp.bfloat16

EPS = 1e-6
LANES = 128
CHUNK_SHIFT = 6
MLA_HEADS = 8
MLA_NOPE = 64
MLA_ROPE = 32
MLA_QK = MLA_NOPE + MLA_ROPE
MLA_V = 64
Q_LORA = 256
KV_LORA = 128
RNN_WIDTH = 512
RNN_BLOCKS = 8
CONV_WIDTH = 4
LRU_C = 8.0
ROPE_BASE = 10000.0
MEM_HEADS = 4
MEM_HEAD_DIM = 256
N_GROUPS = 4
EXPERTS_PER_GROUP = 8
N_EXPERTS = 32
D_EXPERT = 256
ROUTER_OFF = 8
MOE_CHUNK = 256
NEG = -1e30
LOG2E = 1.4426950408889634
VMEM_LIMIT = 56 * 1024 * 1024


def _rms(x, g):
    return x * lax.rsqrt(jnp.mean(x * x, axis=-1, keepdims=True) + EPS) * g


def _bdot(a, b):
    return jnp.dot(a.astype(BF16), b, preferred_element_type=F32)


def _cparams(sem):
    return pltpu.CompilerParams(dimension_semantics=sem, vmem_limit_bytes=VMEM_LIMIT)


def _const_spec(shape):
    return pl.BlockSpec(shape, lambda *_: (0,) * len(shape))


def _proj_kernel(x_ref, gmix_ref, win_ref, gcq_ref, wuq_ref, gckv_ref, wukv_ref, gkn_ref,
                 tq_ref, ck_ref, s1k_ref, s2k_ref, q_ref, k_ref, v_ref, ug_ref, ux_ref, *, nsplit):
    tm = x_ref.shape[1]
    rs = tm // nsplit
    G = tm // 8
    lane = lax.broadcasted_iota(jnp.int32, (1, LANES), 1)
    ones_col = (lane == MLA_V).astype(F32)
    qmask = (lane < MLA_QK).astype(F32)
    inv = 1.0 / MLA_QK
    half = MLA_ROPE // 2
    gkn = gkn_ref[...]
    def project(part):
        rows = slice(part * rs, (part + 1) * rs)
        return _bdot(_rms(x_ref[0, rows, :], gmix_ref[...]), win_ref[...])

    def finish(part, z):
        rows = slice(part * rs, (part + 1) * rs)
        for sg in range(rs // G):
            seg_rows = pl.ds(part * (rs // G) + sg, G, stride=8)
            zs = z[sg * G:(sg + 1) * G]
            gate = jax.nn.gelu(zs[:, 512:1024], approximate=True)
            for c in range(RNN_WIDTH // LANES):
                ug_ref[0, c, seg_rows, :] = gate[:, c * LANES:(c + 1) * LANES]
                ux_ref[0, c, seg_rows, :] = zs[:, 1024 + c * LANES:1024 + (c + 1) * LANES]
        q = _bdot(_rms(z[:, 0:Q_LORA], gcq_ref[...]), wuq_ref[...])
        kv = _bdot(_rms(z[:, Q_LORA:Q_LORA + KV_LORA], gckv_ref[...]), wukv_ref[...])
        kr = z[:, 384:512]
        ss_rope = jnp.sum(kr * kr, axis=-1, keepdims=True)
        rot = (kr * ck_ref[rows, :] + pltpu.roll(kr, LANES - half, axis=1) * s1k_ref[rows, :]
               + pltpu.roll(kr, half, axis=1) * s2k_ref[rows, :])
        rot2 = rot + pltpu.roll(rot, MLA_ROPE, axis=1)
        tq = tq_ref[rows, :]
        for h in range(MLA_HEADS):
            sl = slice(h * LANES, (h + 1) * LANES)
            qh = q[:, sl]
            rq = lax.rsqrt(jnp.sum(qh * qh * qmask, axis=-1, keepdims=True) * inv + EPS)
            q_ref[0, h, rows, :] = (qh * rq * tq).astype(BF16)
            kn = kv[:, sl]
            rk = lax.rsqrt((jnp.sum(kn * kn, axis=-1, keepdims=True) + ss_rope) * inv + EPS)
            k_ref[0, h, rows, :] = ((kn * gkn + rot2) * rk).astype(BF16)
            vsl = slice((MLA_HEADS + h) * LANES, (MLA_HEADS + h + 1) * LANES)
            v_ref[0, h, rows, :] = (kv[:, vsl] + ones_col).astype(BF16)

    for part in range(nsplit):
        finish(part, project(part))


def _proj_call(x, gmix, win, gcq, wuq, gckv, wukv, gkn, tabs, tm):
    B, S, D = x.shape
    H = MLA_HEADS
    hs = jax.ShapeDtypeStruct((B, H, S, LANES), BF16)
    us = jax.ShapeDtypeStruct((B, RNN_WIDTH // LANES, S, LANES), F32)
    hspec = pl.BlockSpec((1, H, tm, LANES), lambda b, i: (b, 0, i, 0))
    uspec = pl.BlockSpec((1, RNN_WIDTH // LANES, tm, LANES), lambda b, i: (b, 0, i, 0))
    tspec = pl.BlockSpec((tm, LANES), lambda b, i: (i, 0))
    return pl.pallas_call(
        functools.partial(_proj_kernel, nsplit=2 if tm % 16 == 0 else 1),
        grid=(B, S // tm),
        in_specs=[pl.BlockSpec((1, tm, D), lambda b, i: (b, i, 0)),
                  _const_spec(gmix.shape), _const_spec(win.shape), _const_spec(gcq.shape),
                  _const_spec(wuq.shape), _const_spec(gckv.shape), _const_spec(wukv.shape),
                  _const_spec(gkn.shape), tspec, tspec, tspec, tspec],
        out_specs=[hspec, hspec, hspec, uspec, uspec],
        out_shape=[hs, hs, hs, us, us],
        compiler_params=_cparams(("parallel", "parallel")),
        name="mla_rglru_in_proj",
    )(x, gmix, win, gcq, wuq, gckv, wukv, gkn, *tabs)


def _rglru_kernel(ug_ref, ux_ref, cw_ref, cb_ref, wrg_ref, brg_ref, wig_ref, big_ref, lam_ref, gout_ref,
                  o_ref, halo_s, hc_s, o_s, *, tm):
    W = RNN_WIDTH
    G = tm // 8
    nh = CONV_WIDTH - 1

    @pl.when(pl.program_id(1) == 0)
    def _():
        halo_s[...] = jnp.zeros((nh, 8, W), F32)
        hc_s[...] = jnp.zeros((8, W), F32)

    nlt = W // LANES
    u3 = jnp.concatenate([ux_ref[0, c] for c in range(nlt)], axis=-1).reshape(G, 8, W)
    row3 = lax.broadcasted_iota(jnp.int32, (nh, 8, W), 1)
    head = pltpu.roll(jnp.where(row3 == 7, halo_s[...], u3[G - nh:]), 1, axis=1)
    halo_s[...] = u3[G - nh:]
    ext = jnp.concatenate([head, u3], axis=0)
    cw = cw_ref[...]
    xc3 = cb_ref[...] + sum(ext[j:j + G] * cw[j:j + 1, :] for j in range(CONV_WIDTH))
    xc = xc3.reshape(tm, W)
    sigmoid = lambda y: 0.5 * jnp.tanh(0.5 * y) + 0.5
    r = sigmoid(_bdot(xc, wrg_ref[...]) + brg_ref[...])
    i = sigmoid(_bdot(xc, wig_ref[...]) + big_ref[...])
    nl = -lam_ref[...]
    softplus = jnp.maximum(nl, 0.0) + jnp.log(1.0 + jnp.exp(-jnp.abs(nl)))
    a = jnp.exp((-LRU_C * r) * softplus)
    om = 1.0 - a * a
    bt = jnp.where(om > 0.0, om * lax.rsqrt(om), 0.0) * (i * xc)
    a3 = a.reshape(G, 8, W)
    b3 = bt.reshape(G, 8, W)

    h = jnp.zeros((8, W), F32)
    p = jnp.ones((8, W), F32)
    hs, ps = [], []
    for g in range(G):
        h = a3[g] * h + b3[g]
        p = p * a3[g]
        hs.append(h)
        ps.append(p)
    rowi = lax.broadcasted_iota(jnp.int32, (8, W), 0)
    s0 = jnp.where(rowi == 0, hc_s[...], 0.0)
    for rr in range(1, 8):
        s0 = jnp.where(rowi == rr, pltpu.roll(h + p * s0, 1, axis=0), s0)
    hc_s[...] = jnp.broadcast_to((h + p * s0)[7:8, :], (8, W))
    h3 = jnp.stack(hs, axis=0) + jnp.stack(ps, axis=0) * s0[None]
    o = jnp.concatenate([ug_ref[0, c] for c in range(nlt)], axis=-1) * h3.reshape(tm, W)
    on = _rms(o, gout_ref[...])
    for c in range(nlt):
        o_s[c] = on[:, c * LANES:(c + 1) * LANES]
    for seg in range(8):
        o_ref[0, seg * G:(seg + 1) * G, :] = jnp.concatenate(
            [o_s[c, pl.ds(seg, G, stride=8), :] for c in range(nlt)], axis=-1).astype(BF16)


def _rglru_call(ug, ux, cw, cb, wrg, brg, wig, big, lam, gout, tm):
    B, nlt, S, _ = ux.shape
    W = nlt * LANES
    uspec = pl.BlockSpec((1, nlt, tm, LANES), lambda b, i: (b, 0, i, 0))
    consts = (cw, cb, wrg, brg, wig, big, lam, gout)
    return pl.pallas_call(
        functools.partial(_rglru_kernel, tm=tm),
        grid=(B, S // tm),
        in_specs=[uspec, uspec] + [_const_spec(c.shape) for c in consts],
        out_specs=pl.BlockSpec((1, tm, W), lambda b, i: (b, i, 0)),
        out_shape=jax.ShapeDtypeStruct((B, S, W), BF16),
        scratch_shapes=[pltpu.VMEM((CONV_WIDTH - 1, 8, W), F32), pltpu.VMEM((8, W), F32),
                        pltpu.VMEM((W // LANES, tm, LANES), F32)],
        compiler_params=_cparams(("parallel", "arbitrary")),
        name="rglru",
    )(ug, ux, *consts)


def _attn_kernel(q_ref, k_ref, v_ref, bias_ref, o_ref, m_s, acc_s, *, tq, tk, hp):
    qi = pl.program_id(2)
    nsub = tq // tk
    m_s[...] = jnp.full(m_s.shape, NEG, F32)
    acc_s[...] = jnp.zeros(acc_s.shape, F32)

    def scores(hh, k0, ksz, r0, nrows):
        return lax.dot_general(q_ref[0, hh, r0:r0 + nrows, :], k_ref[0, hh, pl.ds(k0, ksz), :],
                               (((1,), (1,)), ((), ())), preferred_element_type=F32)

    def update(s, hh, k0, ksz, r0, nrows, masked):
        rows = slice(r0, r0 + nrows)
        if masked:
            tail = s[:, ksz - tk:] + bias_ref[...]
            s = tail if ksz == tk else jnp.concatenate([s[:, :ksz - tk], tail], axis=1)
        m_prev = m_s[hh, rows, :]
        m_next = jnp.maximum(m_prev, jnp.max(s, axis=1, keepdims=True))
        alpha = jnp.exp2(m_prev - m_next)
        p = jnp.exp2(s - jnp.concatenate([m_next] * (ksz // LANES), axis=1))
        pv = jnp.dot(p.astype(BF16), v_ref[0, hh, pl.ds(k0, ksz), :], preferred_element_type=F32)
        acc_s[hh, rows, :] = acc_s[hh, rows, :] * alpha + pv
        m_s[hh, rows, :] = m_next

    def run(blocks):
        s_next = scores(*blocks[0][:5])
        for n, blk in enumerate(blocks):
            s = s_next
            if n + 1 < len(blocks):
                s_next = scores(*blocks[n + 1][:5])
            update(s, *blk)

    def full_body(kb, carry):
        k0 = pl.multiple_of(kb * tq, tq)
        for hh in range(hp):
            update(scores(hh, k0, tq, 0, tq), hh, k0, tq, 0, tq, False)
        return carry

    lax.fori_loop(0, qi, full_body, 0)
    k0 = pl.multiple_of(qi * tq, tq)
    run([(hh, k0, (j + 1) * tk, j * tk, tk, True) for j in range(nsub) for hh in range(hp)])
    lane = lax.broadcasted_iota(jnp.int32, (tq, LANES), 1)
    for pr in range(hp // 2):
        a0, a1 = acc_s[2 * pr], acc_s[2 * pr + 1]
        o0 = a0 / a0[:, MLA_V:MLA_V + 1]
        o1 = a1 / a1[:, MLA_V:MLA_V + 1]
        o_ref[0, :, pr * LANES:(pr + 1) * LANES] = jnp.where(
            lane < MLA_V, o0, pltpu.roll(o1, MLA_V, axis=1)).astype(BF16)


def _attn_call(q, k, v, tq, tk, hp):
    B, H, S, _ = q.shape
    idx = jnp.arange(tk, dtype=jnp.int32) >> CHUNK_SHIFT
    bias = jnp.where(idx[None, :] <= idx[:, None], 0.0, NEG).astype(F32)
    return pl.pallas_call(
        functools.partial(_attn_kernel, tq=tq, tk=tk, hp=hp),
        grid=(B, H // hp, S // tq),
        in_specs=[pl.BlockSpec((1, hp, tq, LANES), lambda b, h, i: (b, h, i, 0)),
                  pl.BlockSpec((1, hp, S, LANES), lambda b, h, i: (b, h, 0, 0)),
                  pl.BlockSpec((1, hp, S, LANES), lambda b, h, i: (b, h, 0, 0)),
                  _const_spec(bias.shape)],
        out_specs=pl.BlockSpec((1, tq, hp * MLA_V), lambda b, h, i: (b, i, h)),
        out_shape=jax.ShapeDtypeStruct((B, S, H * MLA_V), BF16),
        scratch_shapes=[pltpu.VMEM((hp, tq, LANES), F32), pltpu.VMEM((hp, tq, LANES), F32)],
        compiler_params=_cparams(("parallel", "parallel", "arbitrary")),
        name="block_causal_attention",
    )(q, k, v, bias)


def _memkv_kernel(mem_ref, gmem_ref, wmk_ref, wmv_ref, gmkn_ref, k_ref, v_ref):
    m = _rms(mem_ref[0], gmem_ref[...]).astype(BF16)
    k = jnp.dot(m, wmk_ref[...], preferred_element_type=F32)
    v = jnp.dot(m, wmv_ref[...], preferred_element_type=F32)
    g = gmkn_ref[...]
    for h in range(MEM_HEADS):
        sl = slice(h * MEM_HEAD_DIM, (h + 1) * MEM_HEAD_DIM)
        k_ref[0, :, sl] = _rms(k[:, sl], g).astype(BF16)
    v_ref[0] = v.astype(BF16)


def _memkv_call(mem, gmem, wmk, wmv, gmkn):
    B, M, D = mem.shape
    spec = pl.BlockSpec((1, M, D), lambda b: (b, 0, 0))
    os_ = jax.ShapeDtypeStruct((B, M, D), BF16)
    return pl.pallas_call(
        _memkv_kernel,
        grid=(B,),
        in_specs=[spec, _const_spec(gmem.shape), _const_spec(wmk.shape), _const_spec(wmv.shape),
                  _const_spec(gmkn.shape)],
        out_specs=[spec, spec],
        out_shape=[os_, os_],
        compiler_params=_cparams(("parallel",)),
        name="mem_kv",
    )(mem, gmem, wmk, wmv, gmkn)


def _mix_mem_kernel(x_ref, oa_ref, orn_ref, gao_ref, wout_ref, gxq_ref, wmq_ref, gmqn_ref, km_ref, vm_ref,
                    wmo_ref, o_ref):
    oa = _rms(oa_ref[0].astype(F32), gao_ref[...]).astype(BF16)
    mix = jnp.concatenate([oa, orn_ref[0]], axis=-1)
    x1 = x_ref[0] + jnp.dot(mix, wout_ref[...], preferred_element_type=F32)
    qm = _bdot(_rms(x1, gxq_ref[...]), wmq_ref[...])
    g = gmqn_ref[...]
    scale = MEM_HEAD_DIM ** -0.5

    def mem_scores(h):
        sl = slice(h * MEM_HEAD_DIM, (h + 1) * MEM_HEAD_DIM)
        qh = (_rms(qm[:, sl], g) * scale).astype(BF16)
        return lax.dot_general(qh, km_ref[0, :, sl], (((1,), (1,)), ((), ())), preferred_element_type=F32)

    heads = []
    s_next = mem_scores(0)
    for h in range(MEM_HEADS):
        sl = slice(h * MEM_HEAD_DIM, (h + 1) * MEM_HEAD_DIM)
        s = s_next
        if h + 1 < MEM_HEADS:
            s_next = mem_scores(h + 1)
        p = jnp.exp(s - jnp.max(s, axis=-1, keepdims=True))
        oh = jnp.dot(p.astype(BF16), vm_ref[0, :, sl], preferred_element_type=F32)
        heads.append((oh / jnp.sum(p, axis=-1, keepdims=True)).astype(BF16))
    o_ref[0] = x1 + jnp.dot(jnp.concatenate(heads, axis=-1), wmo_ref[...], preferred_element_type=F32)


def _mix_mem_call(x, oa, orn, gao, wout, gxq, wmq, gmqn, km, vm, wmo, tm):
    B, S, D = x.shape
    M = km.shape[1]
    xspec = pl.BlockSpec((1, tm, D), lambda b, i: (b, i, 0))
    hspec = pl.BlockSpec((1, tm, D // 2), lambda b, i: (b, i, 0))
    mspec = pl.BlockSpec((1, M, D), lambda b, i: (b, 0, 0))
    return pl.pallas_call(
        _mix_mem_kernel,
        grid=(B, S // tm),
        in_specs=[xspec, hspec, hspec, _const_spec(gao.shape), _const_spec(wout.shape), _const_spec(gxq.shape),
                  _const_spec(wmq.shape), _const_spec(gmqn.shape), mspec, mspec, _const_spec(wmo.shape)],
        out_specs=xspec,
        out_shape=jax.ShapeDtypeStruct((B, S, D), F32),
        compiler_params=_cparams(("parallel", "parallel")),
        name="out_proj_mem_attn",
    )(x, oa, orn, gao, wout, gxq, wmq, gmqn, km, vm, wmo)


def _route_t(lt):
    tm = lt.shape[1]
    srow = lax.broadcasted_iota(jnp.int32, (EXPERTS_PER_GROUP, tm), 0)
    glog = jnp.where(srow < N_GROUPS, lt[0:8], NEG)
    gmax = jnp.max(glog, axis=0, keepdims=True)
    p_g = 1.0 / jnp.sum(jnp.exp(glog - gmax), axis=0, keepdims=True)
    g_idx = jnp.min(jnp.where(glog == gmax, srow, 8), axis=0, keepdims=True)
    el = lt[ROUTER_OFF:ROUTER_OFF + EXPERTS_PER_GROUP]
    for gg in range(1, N_GROUPS):
        lo = ROUTER_OFF + gg * EXPERTS_PER_GROUP
        el = jnp.where(g_idx == gg, lt[lo:lo + EXPERTS_PER_GROUP], el)
    ee = jnp.exp(el - jnp.max(el, axis=0, keepdims=True))
    probs = ee / jnp.sum(ee, axis=0, keepdims=True)
    v1 = jnp.max(probs, axis=0, keepdims=True)
    i1 = jnp.min(jnp.where(probs == v1, srow, 8), axis=0, keepdims=True)
    rest = srow != i1
    v2 = jnp.max(jnp.where(rest, probs, -1.0), axis=0, keepdims=True)
    i2 = jnp.min(jnp.where(rest & (probs == v2), srow, 8), axis=0, keepdims=True)
    tot = v1 + v2
    gl = jnp.where(srow == i1, p_g * (v1 / tot), 0.0) + jnp.where(srow == i2, p_g * (v2 / tot), 0.0)
    return gl, g_idx


def _moe_kernel(x_ref, gffn_ref, wr_ref, br_ref, wgu_ref, wd_ref, o_ref, text_s, keyc_s, metar_s, cnt_s, *, ch):
    g = pl.program_id(1)
    tm, D = x_ref.shape

    @pl.when(g == 0)
    def _():
        x = x_ref[...]
        t = _rms(x, gffn_ref[...])
        thi = t.astype(BF16)
        tlo = (t - thi.astype(F32)).astype(BF16)
        l1 = jnp.dot(thi, wr_ref[...], preferred_element_type=F32)
        l2 = jnp.dot(tlo, wr_ref[:, 0:LANES], preferred_element_type=F32)
        logits = l1[:, 0:LANES] + l1[:, LANES:] + l2 + br_ref[...]
        gl, gid = _route_t(logits.T)
        srow = lax.broadcasted_iota(jnp.int32, (8, tm), 0)
        oh = srow == gid
        upper = lax.broadcasted_iota(jnp.int32, (tm, tm), 0) < lax.broadcasted_iota(jnp.int32, (tm, tm), 1)
        pref = jnp.dot(oh.astype(BF16), upper.astype(BF16), preferred_element_type=F32)
        rank = jnp.sum(jnp.where(oh, pref, 0.0), axis=0, keepdims=True)
        rg = jnp.where(srow == 0, rank, jnp.where(srow == 1, gid.astype(F32), 0.0))
        metar_s[...] = rg
        keyc_s[0] = jnp.broadcast_to(rank, (LANES, tm)).T
        keyc_s[1] = jnp.broadcast_to(gid.astype(F32), (LANES, tm)).T
        info = jnp.concatenate([gl, jnp.zeros((LANES - 8, tm), F32)], axis=0).T
        ghi = info.astype(BF16)
        text_s[:, 0:D] = thi
        text_s[:, D:D + LANES] = ghi
        text_s[:, D + LANES:] = (info - ghi.astype(F32)).astype(BF16)
        for gg in range(N_GROUPS):
            cnt_s[gg] = jnp.sum(oh[gg:gg + 1, :].astype(jnp.int32))
        o_ref[...] = x

    gf = g.astype(F32)
    nch = (cnt_s[g] + (ch - 1)) // ch
    key_row = jnp.where(metar_s[1:2, :] == gf, metar_s[0:1, :], -1.0).astype(jnp.int32)
    kc = jnp.where(keyc_s[1] == gf, keyc_s[0], -1.0).astype(jnp.int32)
    key_col = jnp.concatenate([kc] * (ch // LANES), axis=1)
    sub = lax.broadcasted_iota(jnp.int32, (ch, tm), 0)
    lan = lax.broadcasted_iota(jnp.int32, (tm, ch), 1)

    def chunk(c, carry):
        base = c * ch
        disp = (key_row - base == sub).astype(BF16)
        xc = jnp.dot(disp, text_s[...], preferred_element_type=F32)
        xb = xc[:, :D].astype(BF16)
        gat = xc[:, D:D + LANES] + xc[:, D + LANES:]
        y = jnp.zeros((ch, D), F32)
        ahead = 1
        hq = [jnp.dot(xb, wgu_ref[e], preferred_element_type=F32) for e in range(ahead)]
        for e in range(EXPERTS_PER_GROUP):
            h = hq.pop(0)
            if e + ahead < EXPERTS_PER_GROUP:
                hq.append(jnp.dot(xb, wgu_ref[e + ahead], preferred_element_type=F32))
            he = jax.nn.silu(h[:, :D_EXPERT]) * h[:, D_EXPERT:] * gat[:, e:e + 1]
            y = y + _bdot(he, wd_ref[e])
        comb = (key_col - base == lan).astype(BF16)
        o_ref[...] += jnp.dot(comb, y.astype(BF16), preferred_element_type=F32)
        return carry

    lax.fori_loop(0, nch, chunk, 0)


def _moe_call(x2, gffn, wr, br, wgu, wd, tm, ch):
    N, D = x2.shape
    xspec = pl.BlockSpec((tm, D), lambda i, g: (i, 0))
    epg = EXPERTS_PER_GROUP
    return pl.pallas_call(
        functools.partial(_moe_kernel, ch=ch),
        grid=(N // tm, N_GROUPS),
        in_specs=[xspec, _const_spec(gffn.shape), _const_spec(wr.shape), _const_spec(br.shape),
                  pl.BlockSpec((epg, D, 2 * D_EXPERT), lambda i, g: (g, 0, 0)),
                  pl.BlockSpec((epg, D_EXPERT, D), lambda i, g: (g, 0, 0))],
        out_specs=xspec,
        out_shape=jax.ShapeDtypeStruct((N, D), F32),
        scratch_shapes=[pltpu.VMEM((tm, D + 2 * LANES), BF16), pltpu.VMEM((2, tm, LANES), F32),
                        pltpu.VMEM((8, tm), F32), pltpu.SMEM((N_GROUPS,), jnp.int32)],
        compiler_params=_cparams(("parallel", "arbitrary")),
        name="hier_moe_grouped",
    )(x2, gffn, wr, br, wgu, wd)


def _pad_lanes(w, width):
    return jnp.pad(w, [(0, 0)] * (w.ndim - 1) + [(0, width - w.shape[-1])])


def _rope_cos_sin(S):
    pos = jnp.arange(S, dtype=F32)
    inv_freq = ROPE_BASE ** (-jnp.arange(0, MLA_ROPE, 2, dtype=F32) / MLA_ROPE)
    ang = pos[:, None] * inv_freq[None, :]
    return jnp.cos(ang), jnp.sin(ang)


def _rope_tables(cos, sin, gq, gk):
    S = cos.shape[0]
    n, half = MLA_NOPE, MLA_ROPE // 2
    z = lambda w: jnp.zeros((S, w), F32)
    gq_ext = jnp.concatenate([gq[:n], gq[n:n + half], gq[n + half:], gq[n + half:], gq[n:n + half]])
    scale = MLA_QK ** -0.5 * LOG2E
    tq = jnp.concatenate([jnp.ones((S, n), F32), cos, cos, -sin, sin], axis=1) * (scale * gq_ext)[None, :]
    gk_rope = jnp.concatenate([jnp.zeros((n,), F32), gk[n:], jnp.zeros((LANES - MLA_QK,), F32)])
    ck = jnp.concatenate([z(n), cos, cos, z(LANES - MLA_QK)], axis=1) * gk_rope[None, :]
    s1k = jnp.concatenate([z(n), -sin, z(LANES - n - half)], axis=1) * jnp.roll(gk_rope, -half)[None, :]
    s2k = jnp.concatenate([z(n + half), sin, z(LANES - MLA_QK)], axis=1) * jnp.roll(gk_rope, half)[None, :]
    return tq, ck, s1k, s2k


def _block_diag(w):
    n, c, d = w.shape
    eye = jnp.eye(n, dtype=w.dtype)
    return (eye[:, None, :, None] * w[:, :, None, :]).reshape(n * c, n * d)


def kernel(x, mem, g_mix, w_in, g_cq, w_uq, g_ckv, w_ukv, g_qn, g_kn, conv_w, conv_b, w_rg, b_rg, w_ig, b_ig,
           lam, g_attn_out, g_rnn_out, w_out, g_xq, g_mem, w_mq, w_mk, w_mv, g_mqn, g_mkn, w_mo, g_ffn,
           w_group, b_group, w_expert, b_expert, w_e_gate, w_e_up, w_e_down):
    B, S, D = x.shape
    H = MLA_HEADS
    tm = min(1024, S)
    tk = min(512, S)
    tq = min(1024, S)
    row = lambda a: a.reshape(1, -1)
    cos, sin = _rope_cos_sin(S)
    for l in range(g_mix.shape[0]):
        wi = w_in[l]
        c0, c1, c2 = Q_LORA + KV_LORA, Q_LORA + KV_LORA + MLA_ROPE, Q_LORA + KV_LORA + MLA_ROPE + RNN_WIDTH
        zc = lambda n: jnp.zeros((D, n), wi.dtype)
        win = jnp.concatenate([wi[:, :c0], zc(MLA_NOPE), wi[:, c0:c1], zc(LANES - MLA_QK), wi[:, c1:c2], wi[:, c2:]],
                              axis=1).astype(BF16)
        wq = w_uq[l].reshape(Q_LORA, H, MLA_QK)
        n, half = MLA_NOPE, MLA_ROPE // 2
        wuq = jnp.concatenate([wq, wq[:, :, n + half:], wq[:, :, n:n + half]], axis=-1).reshape(Q_LORA, H * LANES)
        wuq = wuq.astype(BF16)
        wkv = w_ukv[l].reshape(KV_LORA, H, MLA_NOPE + MLA_V)
        wukv = jnp.concatenate([_pad_lanes(wkv[:, :, :MLA_NOPE], LANES).reshape(KV_LORA, H * LANES),
                                _pad_lanes(wkv[:, :, MLA_NOPE:], LANES).reshape(KV_LORA, H * LANES)],
                               axis=1).astype(BF16)
        gkn = _pad_lanes(row(g_kn[l][:MLA_NOPE]), LANES)
        tabs = _rope_tables(cos, sin, g_qn[l], g_kn[l])
        wrg = _block_diag(w_rg[l]).astype(BF16)
        wig = _block_diag(w_ig[l]).astype(BF16)
        gpad = ROUTER_OFF - N_GROUPS
        wr32 = _pad_lanes(jnp.concatenate([_pad_lanes(w_group[l], ROUTER_OFF), w_expert[l]], axis=1), LANES)
        wr_hi = wr32.astype(BF16)
        wr = jnp.concatenate([wr_hi, (wr32 - wr_hi.astype(F32)).astype(BF16)], axis=1)
        br = _pad_lanes(row(jnp.concatenate([b_group[l], jnp.zeros((gpad,), F32), b_expert[l]])), LANES)
        wgu = jnp.concatenate([w_e_gate[l], w_e_up[l]], axis=-1).astype(BF16)
        wd = w_e_down[l].astype(BF16)

        q, k, v, ug, ux = _proj_call(x, row(g_mix[l]), win, row(g_cq[l]), wuq, row(g_ckv[l]), wukv, gkn, tabs, tm)
        o_rnn = _rglru_call(ug, ux, conv_w[l], row(conv_b[l]), wrg, row(b_rg[l]), wig, row(b_ig[l]), row(lam[l]),
                            row(g_rnn_out[l]), tm)
        o_attn = _attn_call(q, k, v, tq, tk, 4)
        km, vm = _memkv_call(mem, row(g_mem[l]), w_mk[l].astype(BF16), w_mv[l].astype(BF16), row(g_mkn[l]))
        x2 = _mix_mem_call(x, o_attn, o_rnn, row(g_attn_out[l]), w_out[l].astype(BF16), row(g_xq[l]),
                           w_mq[l].astype(BF16), row(g_mqn[l]), km, vm, w_mo[l].astype(BF16), tm)
        x = _moe_call(x2.reshape(B * S, D), row(g_ffn[l]), wr, br, wgu, wd, min(1024, B * S), MOE_CHUNK).reshape(B, S, D)
    return x
```

```python
import functools

import jax
import jax.numpy as jnp
from jax import lax
from jax.experimental import pallas as pl
from jax.experimental.pallas import tpu as pltpu

F32 = jnp.float32
BF16 = jnp.bfloat16

EPS = 1e-6
LANES = 128
CHUNK_SHIFT = 6
MLA_HEADS = 8
MLA_NOPE = 64
MLA_ROPE = 32
MLA_QK = MLA_NOPE + MLA_ROPE
MLA_V = 64
Q_LORA = 256
KV_LORA = 128
RNN_WIDTH = 512
RNN_BLOCKS = 8
CONV_WIDTH = 4
LRU_C = 8.0
ROPE_BASE = 10000.0
MEM_HEADS = 4
MEM_HEAD_DIM = 256
N_GROUPS = 4
EXPERTS_PER_GROUP = 8
N_EXPERTS = 32
D_EXPERT = 256
ROUTER_OFF = 8
MOE_CHUNK = 256
NEG = -1e30
LOG2E = 1.4426950408889634
VMEM_LIMIT = 56 * 1024 * 1024


def _rms(x, g):
    return x * lax.rsqrt(jnp.mean(x * x, axis=-1, keepdims=True) + EPS) * g


def _bdot(a, b):
    return jnp.dot(a.astype(BF16), b, preferred_element_type=F32)


def _cparams(sem):
    return pltpu.CompilerParams(dimension_semantics=sem, vmem_limit_bytes=VMEM_LIMIT)


def _const_spec(shape):
    return pl.BlockSpec(shape, lambda *_: (0,) * len(shape))


def _proj_kernel(x_ref, gmix_ref, win_ref, gcq_ref, wuq_ref, gckv_ref, wukv_ref, gkn_ref,
                 tq_ref, ck_ref, s1k_ref, s2k_ref, q_ref, k_ref, v_ref, ug_ref, ux_ref, *, nsplit):
    tm = x_ref.shape[1]
    rs = tm // nsplit
    G = tm // 8
    lane = lax.broadcasted_iota(jnp.int32, (1, LANES), 1)
    ones_col = (lane == MLA_V).astype(F32)
    qmask = (lane < MLA_QK).astype(F32)
    inv = 1.0 / MLA_QK
    half = MLA_ROPE // 2
    gkn = gkn_ref[...]
    def project(part):
        rows = slice(part * rs, (part + 1) * rs)
        return _bdot(_rms(x_ref[0, rows, :], gmix_ref[...]), win_ref[...])

    def finish(part, z):
        rows = slice(part * rs, (part + 1) * rs)
        for sg in range(rs // G):
            seg_rows = pl.ds(part * (rs // G) + sg, G, stride=8)
            zs = z[sg * G:(sg + 1) * G]
            gate = jax.nn.gelu(zs[:, 512:1024], approximate=True)
            for c in range(RNN_WIDTH // LANES):
                ug_ref[0, c, seg_rows, :] = gate[:, c * LANES:(c + 1) * LANES]
                ux_ref[0, c, seg_rows, :] = zs[:, 1024 + c * LANES:1024 + (c + 1) * LANES]
        q = _bdot(_rms(z[:, 0:Q_LORA], gcq_ref[...]), wuq_ref[...])
        kv = _bdot(_rms(z[:, Q_LORA:Q_LORA + KV_LORA], gckv_ref[...]), wukv_ref[...])
        kr = z[:, 384:512]
        ss_rope = jnp.sum(kr * kr, axis=-1, keepdims=True)
        rot = (kr * ck_ref[rows, :] + pltpu.roll(kr, LANES - half, axis=1) * s1k_ref[rows, :]
               + pltpu.roll(kr, half, axis=1) * s2k_ref[rows, :])
        rot2 = rot + pltpu.roll(rot, MLA_ROPE, axis=1)
        tq = tq_ref[rows, :]
        for h in range(MLA_HEADS):
            sl = slice(h * LANES, (h + 1) * LANES)
            qh = q[:, sl]
            rq = lax.rsqrt(jnp.sum(qh * qh * qmask, axis=-1, keepdims=True) * inv + EPS)
            q_ref[0, h, rows, :] = (qh * rq * tq).astype(BF16)
            kn = kv[:, sl]
            rk = lax.rsqrt((jnp.sum(kn * kn, axis=-1, keepdims=True) + ss_rope) * inv + EPS)
            k_ref[0, h, rows, :] = ((kn * gkn + rot2) * rk).astype(BF16)
            vsl = slice((MLA_HEADS + h) * LANES, (MLA_HEADS + h + 1) * LANES)
            v_ref[0, h, rows, :] = (kv[:, vsl] + ones_col).astype(BF16)

    for part in range(nsplit):
        finish(part, project(part))


def _proj_call(x, gmix, win, gcq, wuq, gckv, wukv, gkn, tabs, tm):
    B, S, D = x.shape
    H = MLA_HEADS
    hs = jax.ShapeDtypeStruct((B, H, S, LANES), BF16)
    us = jax.ShapeDtypeStruct((B, RNN_WIDTH // LANES, S, LANES), F32)
    hspec = pl.BlockSpec((1, H, tm, LANES), lambda b, i: (b, 0, i, 0))
    uspec = pl.BlockSpec((1, RNN_WIDTH // LANES, tm, LANES), lambda b, i: (b, 0, i, 0))
    tspec = pl.BlockSpec((tm, LANES), lambda b, i: (i, 0))
    return pl.pallas_call(
        functools.partial(_proj_kernel, nsplit=2 if tm % 16 == 0 else 1),
        grid=(B, S // tm),
        in_specs=[pl.BlockSpec((1, tm, D), lambda b, i: (b, i, 0)),
                  _const_spec(gmix.shape), _const_spec(win.shape), _const_spec(gcq.shape),
                  _const_spec(wuq.shape), _const_spec(gckv.shape), _const_spec(wukv.shape),
                  _const_spec(gkn.shape), tspec, tspec, tspec, tspec],
        out_specs=[hspec, hspec, hspec, uspec, uspec],
        out_shape=[hs, hs, hs, us, us],
        compiler_params=_cparams(("parallel", "parallel")),
        name="mla_rglru_in_proj",
    )(x, gmix, win, gcq, wuq, gckv, wukv, gkn, *tabs)


def _rglru_kernel(ug_ref, ux_ref, cw_ref, cb_ref, wrg_ref, brg_ref, wig_ref, big_ref, lam_ref, gout_ref,
                  o_ref, halo_s, hc_s, o_s, *, tm):
    W = RNN_WIDTH
    G = tm // 8
    nh = CONV_WIDTH - 1

    @pl.when(pl.program_id(1) == 0)
    def _():
        halo_s[...] = jnp.zeros((nh, 8, W), F32)
        hc_s[...] = jnp.zeros((8, W), F32)

    nlt = W // LANES
    u3 = jnp.concatenate([ux_ref[0, c] for c in range(nlt)], axis=-1).reshape(G, 8, W)
    row3 = lax.broadcasted_iota(jnp.int32, (nh, 8, W), 1)
    head = pltpu.roll(jnp.where(row3 == 7, halo_s[...], u3[G - nh:]), 1, axis=1)
    halo_s[...] = u3[G - nh:]
    ext = jnp.concatenate([head, u3], axis=0)
    cw = cw_ref[...]
    xc3 = cb_ref[...] + sum(ext[j:j + G] * cw[j:j + 1, :] for j in range(CONV_WIDTH))
    xc = xc3.reshape(tm, W)
    sigmoid = lambda y: 0.5 * jnp.tanh(0.5 * y) + 0.5
    r = sigmoid(_bdot(xc, wrg_ref[...]) + brg_ref[...])
    i = sigmoid(_bdot(xc, wig_ref[...]) + big_ref[...])
    nl = -lam_ref[...]
    softplus = jnp.maximum(nl, 0.0) + jnp.log(1.0 + jnp.exp(-jnp.abs(nl)))
    a = jnp.exp((-LRU_C * r) * softplus)
    om = 1.0 - a * a
    bt = jnp.where(om > 0.0, om * lax.rsqrt(om), 0.0) * (i * xc)
    a3 = a.reshape(G, 8, W)
    b3 = bt.reshape(G, 8, W)

    h = jnp.zeros((8, W), F32)
    p = jnp.ones((8, W), F32)
    hs, ps = [], []
    for g in range(G):
        h = a3[g] * h + b3[g]
        p = p * a3[g]
        hs.append(h)
        ps.append(p)
    rowi = lax.broadcasted_iota(jnp.int32, (8, W), 0)
    s0 = jnp.where(rowi == 0, hc_s[...], 0.0)
    for rr in range(1, 8):
        s0 = jnp.where(rowi == rr, pltpu.roll(h + p * s0, 1, axis=0), s0)
    hc_s[...] = jnp.broadcast_to((h + p * s0)[7:8, :], (8, W))
    h3 = jnp.stack(hs, axis=0) + jnp.stack(ps, axis=0) * s0[None]
    o = jnp.concatenate([ug_ref[0, c] for c in range(nlt)], axis=-1) * h3.reshape(tm, W)
    on = _rms(o, gout_ref[...])
    for c in range(nlt):
        o_s[c] = on[:, c * LANES:(c + 1) * LANES]
    for seg in range(8):
        o_ref[0, seg * G:(seg + 1) * G, :] = jnp.concatenate(
            [o_s[c, pl.ds(seg, G, stride=8), :] for c in range(nlt)], axis=-1).astype(BF16)


def _rglru_call(ug, ux, cw, cb, wrg, brg, wig, big, lam, gout, tm):
    B, nlt, S, _ = ux.shape
    W = nlt * LANES
    uspec = pl.BlockSpec((1, nlt, tm, LANES), lambda b, i: (b, 0, i, 0))
    consts = (cw, cb, wrg, brg, wig, big, lam, gout)
    return pl.pallas_call(
        functools.partial(_rglru_kernel, tm=tm),
        grid=(B, S // tm),
        in_specs=[uspec, uspec] + [_const_spec(c.shape) for c in consts],
        out_specs=pl.BlockSpec((1, tm, W), lambda b, i: (b, i, 0)),
        out_shape=jax.ShapeDtypeStruct((B, S, W), BF16),
        scratch_shapes=[pltpu.VMEM((CONV_WIDTH - 1, 8, W), F32), pltpu.VMEM((8, W), F32),
                        pltpu.VMEM((W // LANES, tm, LANES), F32)],
        compiler_params=_cparams(("parallel", "arbitrary")),
        name="rglru",
    )(ug, ux, *consts)


def _attn_kernel(q_ref, k_ref, v_ref, bias_ref, o_ref, m_s, acc_s, *, tq, tk, hp):
    qi = pl.program_id(2)
    nsub = tq // tk
    m_s[...] = jnp.full(m_s.shape, NEG, F32)
    acc_s[...] = jnp.zeros(acc_s.shape, F32)

    def scores(hh, k0, ksz, r0, nrows):
        return lax.dot_general(q_ref[0, hh, r0:r0 + nrows, :], k_ref[0, hh, pl.ds(k0, ksz), :],
                               (((1,), (1,)), ((), ())), preferred_element_type=F32)

    def update(s, hh, k0, ksz, r0, nrows, masked):
        rows = slice(r0, r0 + nrows)
        if masked:
            tail = s[:, ksz - tk:] + bias_ref[...]
            s = tail if ksz == tk else jnp.concatenate([s[:, :ksz - tk], tail], axis=1)
        m_prev = m_s[hh, rows, :]
        m_next = jnp.maximum(m_prev, jnp.max(s, axis=1, keepdims=True))
        alpha = jnp.exp2(m_prev - m_next)
        p = jnp.exp2(s - jnp.concatenate([m_next] * (ksz // LANES), axis=1))
        pv = jnp.dot(p.astype(BF16), v_ref[0, hh, pl.ds(k0, ksz), :], preferred_element_type=F32)
        acc_s[hh, rows, :] = acc_s[hh, rows, :] * alpha + pv
        m_s[hh, rows, :] = m_next

    def run(blocks):
        s_next = scores(*blocks[0][:5])
        for n, blk in enumerate(blocks):
            s = s_next
            if n + 1 < len(blocks):
                s_next = scores(*blocks[n + 1][:5])
            update(s, *blk)

    def full_body(kb, carry):
        k0 = pl.multiple_of(kb * tq, tq)
        for hh in range(hp):
            update(scores(hh, k0, tq, 0, tq), hh, k0, tq, 0, tq, False)
        return carry

    lax.fori_loop(0, qi, full_body, 0)
    k0 = pl.multiple_of(qi * tq, tq)
    run([(hh, k0, (j + 1) * tk, j * tk, tk, True) for j in range(nsub) for hh in range(hp)])
    lane = lax.broadcasted_iota(jnp.int32, (tq, LANES), 1)
    for pr in range(hp // 2):
        a0, a1 = acc_s[2 * pr], acc_s[2 * pr + 1]
        o0 = a0 / a0[:, MLA_V:MLA_V + 1]
        o1 = a1 / a1[:, MLA_V:MLA_V + 1]
        o_ref[0, :, pr * LANES:(pr + 1) * LANES] = jnp.where(
            lane < MLA_V, o0, pltpu.roll(o1, MLA_V, axis=1)).astype(BF16)


def _attn_call(q, k, v, tq, tk, hp):
    B, H, S, _ = q.shape
    idx = jnp.arange(tk, dtype=jnp.int32) >> CHUNK_SHIFT
    bias = jnp.where(idx[None, :] <= idx[:, None], 0.0, NEG).astype(F32)
    return pl.pallas_call(
        functools.partial(_attn_kernel, tq=tq, tk=tk, hp=hp),
        grid=(B, H // hp, S // tq),
        in_specs=[pl.BlockSpec((1, hp, tq, LANES), lambda b, h, i: (b, h, i, 0)),
                  pl.BlockSpec((1, hp, S, LANES), lambda b, h, i: (b, h, 0, 0)),
                  pl.BlockSpec((1, hp, S, LANES), lambda b, h, i: (b, h, 0, 0)),
                  _const_spec(bias.shape)],
        out_specs=pl.BlockSpec((1, tq, hp * MLA_V), lambda b, h, i: (b, i, h)),
        out_shape=jax.ShapeDtypeStruct((B, S, H * MLA_V), BF16),
        scratch_shapes=[pltpu.VMEM((hp, tq, LANES), F32), pltpu.VMEM((hp, tq, LANES), F32)],
        compiler_params=_cparams(("parallel", "parallel", "arbitrary")),
        name="block_causal_attention",
    )(q, k, v, bias)


def _memkv_kernel(mem_ref, gmem_ref, wmk_ref, wmv_ref, gmkn_ref, k_ref, v_ref):
    m = _rms(mem_ref[0], gmem_ref[...]).astype(BF16)
    k = jnp.dot(m, wmk_ref[...], preferred_element_type=F32)
    v = jnp.dot(m, wmv_ref[...], preferred_element_type=F32)
    g = gmkn_ref[...]
    for h in range(MEM_HEADS):
        sl = slice(h * MEM_HEAD_DIM, (h + 1) * MEM_HEAD_DIM)
        k_ref[0, :, sl] = _rms(k[:, sl], g).astype(BF16)
    v_ref[0] = v.astype(BF16)


def _memkv_call(mem, gmem, wmk, wmv, gmkn):
    B, M, D = mem.shape
    spec = pl.BlockSpec((1, M, D), lambda b: (b, 0, 0))
    os_ = jax.ShapeDtypeStruct((B, M, D), BF16)
    return pl.pallas_call(
        _memkv_kernel,
        grid=(B,),
        in_specs=[spec, _const_spec(gmem.shape), _const_spec(wmk.shape), _const_spec(wmv.shape),
                  _const_spec(gmkn.shape)],
        out_specs=[spec, spec],
        out_shape=[os_, os_],
        compiler_params=_cparams(("parallel",)),
        name="mem_kv",
    )(mem, gmem, wmk, wmv, gmkn)


def _mix_mem_kernel(x_ref, oa_ref, orn_ref, gao_ref, wout_ref, gxq_ref, wmq_ref, gmqn_ref, km_ref, vm_ref,
                    wmo_ref, o_ref):
    oa = _rms(oa_ref[0].astype(F32), gao_ref[...]).astype(BF16)
    mix = jnp.concatenate([oa, orn_ref[0]], axis=-1)
    x1 = x_ref[0] + jnp.dot(mix, wout_ref[...], preferred_element_type=F32)
    qm = _bdot(_rms(x1, gxq_ref[...]), wmq_ref[...])
    g = gmqn_ref[...]
    scale = MEM_HEAD_DIM ** -0.5

    def mem_scores(h):
        sl = slice(h * MEM_HEAD_DIM, (h + 1) * MEM_HEAD_DIM)
        qh = (_rms(qm[:, sl], g) * scale).astype(BF16)
        return lax.dot_general(qh, km_ref[0, :, sl], (((1,), (1,)), ((), ())), preferred_element_type=F32)

    heads = []
    s_next = mem_scores(0)
    for h in range(MEM_HEADS):
        sl = slice(h * MEM_HEAD_DIM, (h + 1) * MEM_HEAD_DIM)
        s = s_next
        if h + 1 < MEM_HEADS:
            s_next = mem_scores(h + 1)
        p = jnp.exp(s - jnp.max(s, axis=-1, keepdims=True))
        oh = jnp.dot(p.astype(BF16), vm_ref[0, :, sl], preferred_element_type=F32)
        heads.append((oh / jnp.sum(p, axis=-1, keepdims=True)).astype(BF16))
    o_ref[0] = x1 + jnp.dot(jnp.concatenate(heads, axis=-1), wmo_ref[...], preferred_element_type=F32)


def _mix_mem_call(x, oa, orn, gao, wout, gxq, wmq, gmqn, km, vm, wmo, tm):
    B, S, D = x.shape
    M = km.shape[1]
    xspec = pl.BlockSpec((1, tm, D), lambda b, i: (b, i, 0))
    hspec = pl.BlockSpec((1, tm, D // 2), lambda b, i: (b, i, 0))
    mspec = pl.BlockSpec((1, M, D), lambda b, i: (b, 0, 0))
    return pl.pallas_call(
        _mix_mem_kernel,
        grid=(B, S // tm),
        in_specs=[xspec, hspec, hspec, _const_spec(gao.shape), _const_spec(wout.shape), _const_spec(gxq.shape),
                  _const_spec(wmq.shape), _const_spec(gmqn.shape), mspec, mspec, _const_spec(wmo.shape)],
        out_specs=xspec,
        out_shape=jax.ShapeDtypeStruct((B, S, D), F32),
        compiler_params=_cparams(("parallel", "parallel")),
        name="out_proj_mem_attn",
    )(x, oa, orn, gao, wout, gxq, wmq, gmqn, km, vm, wmo)


def _route_t(lt):
    tm = lt.shape[1]
    srow = lax.broadcasted_iota(jnp.int32, (EXPERTS_PER_GROUP, tm), 0)
    glog = jnp.where(srow < N_GROUPS, lt[0:8], NEG)
    gmax = jnp.max(glog, axis=0, keepdims=True)
    p_g = 1.0 / jnp.sum(jnp.exp(glog - gmax), axis=0, keepdims=True)
    g_idx = jnp.min(jnp.where(glog == gmax, srow, 8), axis=0, keepdims=True)
    el = lt[ROUTER_OFF:ROUTER_OFF + EXPERTS_PER_GROUP]
    for gg in range(1, N_GROUPS):
        lo = ROUTER_OFF + gg * EXPERTS_PER_GROUP
        el = jnp.where(g_idx == gg, lt[lo:lo + EXPERTS_PER_GROUP], el)
    ee = jnp.exp(el - jnp.max(el, axis=0, keepdims=True))
    probs = ee / jnp.sum(ee, axis=0, keepdims=True)
    v1 = jnp.max(probs, axis=0, keepdims=True)
    i1 = jnp.min(jnp.where(probs == v1, srow, 8), axis=0, keepdims=True)
    rest = srow != i1
    v2 = jnp.max(jnp.where(rest, probs, -1.0), axis=0, keepdims=True)
    i2 = jnp.min(jnp.where(rest & (probs == v2), srow, 8), axis=0, keepdims=True)
    tot = v1 + v2
    gl = jnp.where(srow == i1, p_g * (v1 / tot), 0.0) + jnp.where(srow == i2, p_g * (v2 / tot), 0.0)
    return gl, g_idx


def _moe_kernel(x_ref, gffn_ref, wr_ref, br_ref, wgu_ref, wd_ref, o_ref, text_s, keyc_s, metar_s, cnt_s, *, ch):
    g = pl.program_id(1)
    tm, D = x_ref.shape

    @pl.when(g == 0)
    def _():
        x = x_ref[...]
        t = _rms(x, gffn_ref[...])
        thi = t.astype(BF16)
        tlo = (t - thi.astype(F32)).astype(BF16)
        l1 = jnp.dot(thi, wr_ref[...], preferred_element_type=F32)
        l2 = jnp.dot(tlo, wr_ref[:, 0:LANES], preferred_element_type=F32)
        logits = l1[:, 0:LANES] + l1[:, LANES:] + l2 + br_ref[...]
        gl, gid = _route_t(logits.T)
        srow = lax.broadcasted_iota(jnp.int32, (8, tm), 0)
        oh = srow == gid
        upper = lax.broadcasted_iota(jnp.int32, (tm, tm), 0) < lax.broadcasted_iota(jnp.int32, (tm, tm), 1)
        pref = jnp.dot(oh.astype(BF16), upper.astype(BF16), preferred_element_type=F32)
        rank = jnp.sum(jnp.where(oh, pref, 0.0), axis=0, keepdims=True)
        rg = jnp.where(srow == 0, rank, jnp.where(srow == 1, gid.astype(F32), 0.0))
        metar_s[...] = rg
        keyc_s[0] = jnp.broadcast_to(rank, (LANES, tm)).T
        keyc_s[1] = jnp.broadcast_to(gid.astype(F32), (LANES, tm)).T
        info = jnp.concatenate([gl, jnp.zeros((LANES - 8, tm), F32)], axis=0).T
        ghi = info.astype(BF16)
        text_s[:, 0:D] = thi
        text_s[:, D:D + LANES] = ghi
        text_s[:, D + LANES:] = (info - ghi.astype(F32)).astype(BF16)
        for gg in range(N_GROUPS):
            cnt_s[gg] = jnp.sum(oh[gg:gg + 1, :].astype(jnp.int32))
        o_ref[...] = x

    gf = g.astype(F32)
    nch = (cnt_s[g] + (ch - 1)) // ch
    key_row = jnp.where(metar_s[1:2, :] == gf, metar_s[0:1, :], -1.0).astype(jnp.int32)
    kc = jnp.where(keyc_s[1] == gf, keyc_s[0], -1.0).astype(jnp.int32)
    key_col = jnp.concatenate([kc] * (ch // LANES), axis=1)
    sub = lax.broadcasted_iota(jnp.int32, (ch, tm), 0)
    lan = lax.broadcasted_iota(jnp.int32, (tm, ch), 1)

    def chunk(c, carry):
        base = c * ch
        disp = (key_row - base == sub).astype(BF16)
        xc = jnp.dot(disp, text_s[...], preferred_element_type=F32)
        xb = xc[:, :D].astype(BF16)
        gat = xc[:, D:D + LANES] + xc[:, D + LANES:]
        y = jnp.zeros((ch, D), F32)
        ahead = 1
        hq = [jnp.dot(xb, wgu_ref[e], preferred_element_type=F32) for e in range(ahead)]
        for e in range(EXPERTS_PER_GROUP):
            h = hq.pop(0)
            if e + ahead < EXPERTS_PER_GROUP:
                hq.append(jnp.dot(xb, wgu_ref[e + ahead], preferred_element_type=F32))
            he = jax.nn.silu(h[:, :D_EXPERT]) * h[:, D_EXPERT:] * gat[:, e:e + 1]
            y = y + _bdot(he, wd_ref[e])
        comb = (key_col - base == lan).astype(BF16)
        o_ref[...] += jnp.dot(comb, y.astype(BF16), preferred_element_type=F32)
        return carry

    lax.fori_loop(0, nch, chunk, 0)


def _moe_call(x2, gffn, wr, br, wgu, wd, tm, ch):
    N, D = x2.shape
    xspec = pl.BlockSpec((tm, D), lambda i, g: (i, 0))
    epg = EXPERTS_PER_GROUP
    return pl.pallas_call(
        functools.partial(_moe_kernel, ch=ch),
        grid=(N // tm, N_GROUPS),
        in_specs=[xspec, _const_spec(gffn.shape), _const_spec(wr.shape), _const_spec(br.shape),
                  pl.BlockSpec((epg, D, 2 * D_EXPERT), lambda i, g: (g, 0, 0)),
                  pl.BlockSpec((epg, D_EXPERT, D), lambda i, g: (g, 0, 0))],
        out_specs=xspec,
        out_shape=jax.ShapeDtypeStruct((N, D), F32),
        scratch_shapes=[pltpu.VMEM((tm, D + 2 * LANES), BF16), pltpu.VMEM((2, tm, LANES), F32),
                        pltpu.VMEM((8, tm), F32), pltpu.SMEM((N_GROUPS,), jnp.int32)],
        compiler_params=_cparams(("parallel", "arbitrary")),
        name="hier_moe_grouped",
    )(x2, gffn, wr, br, wgu, wd)


def _pad_lanes(w, width):
    return jnp.pad(w, [(0, 0)] * (w.ndim - 1) + [(0, width - w.shape[-1])])


def _rope_cos_sin(S):
    pos = jnp.arange(S, dtype=F32)
    inv_freq = ROPE_BASE ** (-jnp.arange(0, MLA_ROPE, 2, dtype=F32) / MLA_ROPE)
    ang = pos[:, None] * inv_freq[None, :]
    return jnp.cos(ang), jnp.sin(ang)


def _rope_tables(cos, sin, gq, gk):
    S = cos.shape[0]
    n, half = MLA_NOPE, MLA_ROPE // 2
    z = lambda w: jnp.zeros((S, w), F32)
    gq_ext = jnp.concatenate([gq[:n], gq[n:n + half], gq[n + half:], gq[n + half:], gq[n:n + half]])
    scale = MLA_QK ** -0.5 * LOG2E
    tq = jnp.concatenate([jnp.ones((S, n), F32), cos, cos, -sin, sin], axis=1) * (scale * gq_ext)[None, :]
    gk_rope = jnp.concatenate([jnp.zeros((n,), F32), gk[n:], jnp.zeros((LANES - MLA_QK,), F32)])
    ck = jnp.concatenate([z(n), cos, cos, z(LANES - MLA_QK)], axis=1) * gk_rope[None, :]
    s1k = jnp.concatenate([z(n), -sin, z(LANES - n - half)], axis=1) * jnp.roll(gk_rope, -half)[None, :]
    s2k = jnp.concatenate([z(n + half), sin, z(LANES - MLA_QK)], axis=1) * jnp.roll(gk_rope, half)[None, :]
    return tq, ck, s1k, s2k


def _block_diag(w):
    n, c, d = w.shape
    eye = jnp.eye(n, dtype=w.dtype)
    return (eye[:, None, :, None] * w[:, :, None, :]).reshape(n * c, n * d)


def kernel(x, mem, g_mix, w_in, g_cq, w_uq, g_ckv, w_ukv, g_qn, g_kn, conv_w, conv_b, w_rg, b_rg, w_ig, b_ig,
           lam, g_attn_out, g_rnn_out, w_out, g_xq, g_mem, w_mq, w_mk, w_mv, g_mqn, g_mkn, w_mo, g_ffn,
           w_group, b_group, w_expert, b_expert, w_e_gate, w_e_up, w_e_down):
    B, S, D = x.shape
    H = MLA_HEADS
    tm = min(1024, S)
    tk = min(256, S)
    tq = min(1024, S)
    row = lambda a: a.reshape(1, -1)
    cos, sin = _rope_cos_sin(S)
    for l in range(g_mix.shape[0]):
        wi = w_in[l]
        c0, c1, c2 = Q_LORA + KV_LORA, Q_LORA + KV_LORA + MLA_ROPE, Q_LORA + KV_LORA + MLA_ROPE + RNN_WIDTH
        zc = lambda n: jnp.zeros((D, n), wi.dtype)
        win = jnp.concatenate([wi[:, :c0], zc(MLA_NOPE), wi[:, c0:c1], zc(LANES - MLA_QK), wi[:, c1:c2], wi[:, c2:]],
                              axis=1).astype(BF16)
        wq = w_uq[l].reshape(Q_LORA, H, MLA_QK)
        n, half = MLA_NOPE, MLA_ROPE // 2
        wuq = jnp.concatenate([wq, wq[:, :, n + half:], wq[:, :, n:n + half]], axis=-1).reshape(Q_LORA, H * LANES)
        wuq = wuq.astype(BF16)
        wkv = w_ukv[l].reshape(KV_LORA, H, MLA_NOPE + MLA_V)
        wukv = jnp.concatenate([_pad_lanes(wkv[:, :, :MLA_NOPE], LANES).reshape(KV_LORA, H * LANES),
                                _pad_lanes(wkv[:, :, MLA_NOPE:], LANES).reshape(KV_LORA, H * LANES)],
                               axis=1).astype(BF16)
        gkn = _pad_lanes(row(g_kn[l][:MLA_NOPE]), LANES)
        tabs = _rope_tables(cos, sin, g_qn[l], g_kn[l])
        wrg = _block_diag(w_rg[l]).astype(BF16)
        wig = _block_diag(w_ig[l]).astype(BF16)
        gpad = ROUTER_OFF - N_GROUPS
        wr32 = _pad_lanes(jnp.concatenate([_pad_lanes(w_group[l], ROUTER_OFF), w_expert[l]], axis=1), LANES)
        wr_hi = wr32.astype(BF16)
        wr = jnp.concatenate([wr_hi, (wr32 - wr_hi.astype(F32)).astype(BF16)], axis=1)
        br = _pad_lanes(row(jnp.concatenate([b_group[l], jnp.zeros((gpad,), F32), b_expert[l]])), LANES)
        wgu = jnp.concatenate([w_e_gate[l], w_e_up[l]], axis=-1).astype(BF16)
        wd = w_e_down[l].astype(BF16)

        q, k, v, ug, ux = _proj_call(x, row(g_mix[l]), win, row(g_cq[l]), wuq, row(g_ckv[l]), wukv, gkn, tabs, tm)
        o_rnn = _rglru_call(ug, ux, conv_w[l], row(conv_b[l]), wrg, row(b_rg[l]), wig, row(b_ig[l]), row(lam[l]),
                            row(g_rnn_out[l]), tm)
        o_attn = _attn_call(q, k, v, tq, tk, 4)
        km, vm = _memkv_call(mem, row(g_mem[l]), w_mk[l].astype(BF16), w_mv[l].astype(BF16), row(g_mkn[l]))
        x2 = _mix_mem_call(x, o_attn, o_rnn, row(g_attn_out[l]), w_out[l].astype(BF16), row(g_xq[l]),
                           w_mq[l].astype(BF16), row(g_mqn[l]), km, vm, w_mo[l].astype(BF16), tm)
        x = _moe_call(x2.reshape(B * S, D), row(g_ffn[l]), wr, br, wgu, wd, min(1024, B * S), MOE_CHUNK).reshape(B, S, D)
    return x
```

```python
import functools

import jax
import jax.numpy as jnp
from jax import lax
from jax.experimental import pallas as pl
from jax.experimental.pallas import tpu as pltpu

F32 = jnp.float32
BF16 = jnp.bfloat16

EPS = 1e-6
LANES = 128
CHUNK_SHIFT = 6
MLA_HEADS = 8
MLA_NOPE = 64
MLA_ROPE = 32
MLA_QK = MLA_NOPE + MLA_ROPE
MLA_V = 64
Q_LORA = 256
KV_LORA = 128
RNN_WIDTH = 512
RNN_BLOCKS = 8
CONV_WIDTH = 4
LRU_C = 8.0
ROPE_BASE = 10000.0
MEM_HEADS = 4
MEM_HEAD_DIM = 256
N_GROUPS = 4
EXPERTS_PER_GROUP = 8
N_EXPERTS = 32
D_EXPERT = 256
ROUTER_OFF = 8
MOE_CHUNK = 256
NEG = -1e30
LOG2E = 1.4426950408889634
VMEM_LIMIT = 56 * 1024 * 1024


def _rms(x, g):
    return x * lax.rsqrt(jnp.mean(x * x, axis=-1, keepdims=True) + EPS) * g


def _bdot(a, b):
    return jnp.dot(a.astype(BF16), b, preferred_element_type=F32)


def _cparams(sem):
    return pltpu.CompilerParams(dimension_semantics=sem, vmem_limit_bytes=VMEM_LIMIT)


def _const_spec(shape):
    return pl.BlockSpec(shape, lambda *_: (0,) * len(shape))


def _proj_kernel(x_ref, gmix_ref, win_ref, gcq_ref, wuq_ref, gckv_ref, wukv_ref, gkn_ref,
                 tq_ref, ck_ref, s1k_ref, s2k_ref, q_ref, k_ref, v_ref, ug_ref, ux_ref, *, nsplit):
    tm = x_ref.shape[1]
    rs = tm // nsplit
    G = tm // 8
    lane = lax.broadcasted_iota(jnp.int32, (1, LANES), 1)
    ones_col = (lane == MLA_V).astype(F32)
    qmask = (lane < MLA_QK).astype(F32)
    inv = 1.0 / MLA_QK
    half = MLA_ROPE // 2
    gkn = gkn_ref[...]
    def project(part):
        rows = slice(part * rs, (part + 1) * rs)
        return _bdot(_rms(x_ref[0, rows, :], gmix_ref[...]), win_ref[...])

    def finish(part, z):
        rows = slice(part * rs, (part + 1) * rs)
        for sg in range(rs // G):
            seg_rows = pl.ds(part * (rs // G) + sg, G, stride=8)
            zs = z[sg * G:(sg + 1) * G]
            gate = jax.nn.gelu(zs[:, 512:1024], approximate=True)
            for c in range(RNN_WIDTH // LANES):
                ug_ref[0, c, seg_rows, :] = gate[:, c * LANES:(c + 1) * LANES]
                ux_ref[0, c, seg_rows, :] = zs[:, 1024 + c * LANES:1024 + (c + 1) * LANES]
        q = _bdot(_rms(z[:, 0:Q_LORA], gcq_ref[...]), wuq_ref[...])
        kv = _bdot(_rms(z[:, Q_LORA:Q_LORA + KV_LORA], gckv_ref[...]), wukv_ref[...])
        kr = z[:, 384:512]
        ss_rope = jnp.sum(kr * kr, axis=-1, keepdims=True)
        rot = (kr * ck_ref[rows, :] + pltpu.roll(kr, LANES - half, axis=1) * s1k_ref[rows, :]
               + pltpu.roll(kr, half, axis=1) * s2k_ref[rows, :])
        rot2 = rot + pltpu.roll(rot, MLA_ROPE, axis=1)
        tq = tq_ref[rows, :]
        for h in range(MLA_HEADS):
            sl = slice(h * LANES, (h + 1) * LANES)
            qh = q[:, sl]
            rq = lax.rsqrt(jnp.sum(qh * qh * qmask, axis=-1, keepdims=True) * inv + EPS)
            q_ref[0, h, rows, :] = (qh * rq * tq).astype(BF16)
            kn = kv[:, sl]
            rk = lax.rsqrt((jnp.sum(kn * kn, axis=-1, keepdims=True) + ss_rope) * inv + EPS)
            k_ref[0, h, rows, :] = ((kn * gkn + rot2) * rk).astype(BF16)
            vsl = slice((MLA_HEADS + h) * LANES, (MLA_HEADS + h + 1) * LANES)
            v_ref[0, h, rows, :] = (kv[:, vsl] + ones_col).astype(BF16)

    for part in range(nsplit):
        finish(part, project(part))


def _proj_call(x, gmix, win, gcq, wuq, gckv, wukv, gkn, tabs, tm):
    B, S, D = x.shape
    H = MLA_HEADS
    hs = jax.ShapeDtypeStruct((B, H, S, LANES), BF16)
    us = jax.ShapeDtypeStruct((B, RNN_WIDTH // LANES, S, LANES), F32)
    hspec = pl.BlockSpec((1, H, tm, LANES), lambda i, b: (b, 0, i, 0))
    uspec = pl.BlockSpec((1, RNN_WIDTH // LANES, tm, LANES), lambda i, b: (b, 0, i, 0))
    tspec = pl.BlockSpec((tm, LANES), lambda i, b: (i, 0))
    return pl.pallas_call(
        functools.partial(_proj_kernel, nsplit=2 if tm % 16 == 0 else 1),
        grid=(S // tm, B),
        in_specs=[pl.BlockSpec((1, tm, D), lambda i, b: (b, i, 0)),
                  _const_spec(gmix.shape), _const_spec(win.shape), _const_spec(gcq.shape),
                  _const_spec(wuq.shape), _const_spec(gckv.shape), _const_spec(wukv.shape),
                  _const_spec(gkn.shape), tspec, tspec, tspec, tspec],
        out_specs=[hspec, hspec, hspec, uspec, uspec],
        out_shape=[hs, hs, hs, us, us],
        compiler_params=_cparams(("parallel", "parallel")),
        name="mla_rglru_in_proj",
    )(x, gmix, win, gcq, wuq, gckv, wukv, gkn, *tabs)


def _rglru_kernel(ug_ref, ux_ref, cw_ref, cb_ref, wrg_ref, brg_ref, wig_ref, big_ref, lam_ref, gout_ref,
                  o_ref, halo_s, hc_s, o_s, *, tm):
    W = RNN_WIDTH
    G = tm // 8
    nh = CONV_WIDTH - 1

    @pl.when(pl.program_id(1) == 0)
    def _():
        halo_s[...] = jnp.zeros((nh, 8, W), F32)
        hc_s[...] = jnp.zeros((8, W), F32)

    nlt = W // LANES
    u3 = jnp.concatenate([ux_ref[0, c] for c in range(nlt)], axis=-1).reshape(G, 8, W)
    row3 = lax.broadcasted_iota(jnp.int32, (nh, 8, W), 1)
    head = pltpu.roll(jnp.where(row3 == 7, halo_s[...], u3[G - nh:]), 1, axis=1)
    halo_s[...] = u3[G - nh:]
    ext = jnp.concatenate([head, u3], axis=0)
    cw = cw_ref[...]
    xc3 = cb_ref[...] + sum(ext[j:j + G] * cw[j:j + 1, :] for j in range(CONV_WIDTH))
    xc = xc3.reshape(tm, W)
    sigmoid = lambda y: 0.5 * jnp.tanh(0.5 * y) + 0.5
    r = sigmoid(_bdot(xc, wrg_ref[...]) + brg_ref[...])
    i = sigmoid(_bdot(xc, wig_ref[...]) + big_ref[...])
    nl = -lam_ref[...]
    softplus = jnp.maximum(nl, 0.0) + jnp.log(1.0 + jnp.exp(-jnp.abs(nl)))
    a = jnp.exp((-LRU_C * r) * softplus)
    om = 1.0 - a * a
    bt = jnp.where(om > 0.0, om * lax.rsqrt(om), 0.0) * (i * xc)
    a3 = a.reshape(G, 8, W)
    b3 = bt.reshape(G, 8, W)

    h = jnp.zeros((8, W), F32)
    p = jnp.ones((8, W), F32)
    hs, ps = [], []
    for g in range(G):
        h = a3[g] * h + b3[g]
        p = p * a3[g]
        hs.append(h)
        ps.append(p)
    rowi = lax.broadcasted_iota(jnp.int32, (8, W), 0)
    s0 = jnp.where(rowi == 0, hc_s[...], 0.0)
    for rr in range(1, 8):
        s0 = jnp.where(rowi == rr, pltpu.roll(h + p * s0, 1, axis=0), s0)
    hc_s[...] = jnp.broadcast_to((h + p * s0)[7:8, :], (8, W))
    h3 = jnp.stack(hs, axis=0) + jnp.stack(ps, axis=0) * s0[None]
    o = jnp.concatenate([ug_ref[0, c] for c in range(nlt)], axis=-1) * h3.reshape(tm, W)
    on = _rms(o, gout_ref[...])
    for c in range(nlt):
        o_s[c] = on[:, c * LANES:(c + 1) * LANES]
    for seg in range(8):
        o_ref[0, seg * G:(seg + 1) * G, :] = jnp.concatenate(
            [o_s[c, pl.ds(seg, G, stride=8), :] for c in range(nlt)], axis=-1).astype(BF16)


def _rglru_call(ug, ux, cw, cb, wrg, brg, wig, big, lam, gout, tm):
    B, nlt, S, _ = ux.shape
    W = nlt * LANES
    uspec = pl.BlockSpec((1, nlt, tm, LANES), lambda b, i: (b, 0, i, 0))
    consts = (cw, cb, wrg, brg, wig, big, lam, gout)
    return pl.pallas_call(
        functools.partial(_rglru_kernel, tm=tm),
        grid=(B, S // tm),
        in_specs=[uspec, uspec] + [_const_spec(c.shape) for c in consts],
        out_specs=pl.BlockSpec((1, tm, W), lambda b, i: (b, i, 0)),
        out_shape=jax.ShapeDtypeStruct((B, S, W), BF16),
        scratch_shapes=[pltpu.VMEM((CONV_WIDTH - 1, 8, W), F32), pltpu.VMEM((8, W), F32),
                        pltpu.VMEM((W // LANES, tm, LANES), F32)],
        compiler_params=_cparams(("parallel", "arbitrary")),
        name="rglru",
    )(ug, ux, *consts)


def _attn_kernel(q_ref, k_ref, v_ref, bias_ref, o_ref, m_s, acc_s, *, tq, tk, hp):
    qi = pl.program_id(2)
    nsub = tq // tk
    m_s[...] = jnp.full(m_s.shape, NEG, F32)
    acc_s[...] = jnp.zeros(acc_s.shape, F32)

    def scores(hh, k0, ksz, r0, nrows):
        return lax.dot_general(q_ref[0, hh, r0:r0 + nrows, :], k_ref[0, hh, pl.ds(k0, ksz), :],
                               (((1,), (1,)), ((), ())), preferred_element_type=F32)

    def update(s, hh, k0, ksz, r0, nrows, masked):
        rows = slice(r0, r0 + nrows)
        if masked:
            tail = s[:, ksz - tk:] + bias_ref[...]
            s = tail if ksz == tk else jnp.concatenate([s[:, :ksz - tk], tail], axis=1)
        m_prev = m_s[hh, rows, :]
        m_next = jnp.maximum(m_prev, jnp.max(s, axis=1, keepdims=True))
        alpha = jnp.exp2(m_prev - m_next)
        p = jnp.exp2(s - jnp.concatenate([m_next] * (ksz // LANES), axis=1))
        pv = jnp.dot(p.astype(BF16), v_ref[0, hh, pl.ds(k0, ksz), :], preferred_element_type=F32)
        acc_s[hh, rows, :] = acc_s[hh, rows, :] * alpha + pv
        m_s[hh, rows, :] = m_next

    def run(blocks):
        s_next = scores(*blocks[0][:5])
        for n, blk in enumerate(blocks):
            s = s_next
            if n + 1 < len(blocks):
                s_next = scores(*blocks[n + 1][:5])
            update(s, *blk)

    def full_body(kb, carry):
        k0 = pl.multiple_of(kb * tq, tq)
        for hh in range(hp):
            update(scores(hh, k0, tq, 0, tq), hh, k0, tq, 0, tq, False)
        return carry

    lax.fori_loop(0, qi, full_body, 0)
    k0 = pl.multiple_of(qi * tq, tq)
    run([(hh, k0, (j + 1) * tk, j * tk, tk, True) for j in range(nsub) for hh in range(hp)])
    lane = lax.broadcasted_iota(jnp.int32, (tq, LANES), 1)
    for pr in range(hp // 2):
        a0, a1 = acc_s[2 * pr], acc_s[2 * pr + 1]
        o0 = a0 / a0[:, MLA_V:MLA_V + 1]
        o1 = a1 / a1[:, MLA_V:MLA_V + 1]
        o_ref[0, :, pr * LANES:(pr + 1) * LANES] = jnp.where(
            lane < MLA_V, o0, pltpu.roll(o1, MLA_V, axis=1)).astype(BF16)


def _attn_call(q, k, v, tq, tk, hp):
    B, H, S, _ = q.shape
    idx = jnp.arange(tk, dtype=jnp.int32) >> CHUNK_SHIFT
    bias = jnp.where(idx[None, :] <= idx[:, None], 0.0, NEG).astype(F32)
    return pl.pallas_call(
        functools.partial(_attn_kernel, tq=tq, tk=tk, hp=hp),
        grid=(B, H // hp, S // tq),
        in_specs=[pl.BlockSpec((1, hp, tq, LANES), lambda b, h, i: (b, h, i, 0)),
                  pl.BlockSpec((1, hp, S, LANES), lambda b, h, i: (b, h, 0, 0)),
                  pl.BlockSpec((1, hp, S, LANES), lambda b, h, i: (b, h, 0, 0)),
                  _const_spec(bias.shape)],
        out_specs=pl.BlockSpec((1, tq, hp * MLA_V), lambda b, h, i: (b, i, h)),
        out_shape=jax.ShapeDtypeStruct((B, S, H * MLA_V), BF16),
        scratch_shapes=[pltpu.VMEM((hp, tq, LANES), F32), pltpu.VMEM((hp, tq, LANES), F32)],
        compiler_params=_cparams(("parallel", "parallel", "arbitrary")),
        name="block_causal_attention",
    )(q, k, v, bias)


def _memkv_kernel(mem_ref, gmem_ref, wmk_ref, wmv_ref, gmkn_ref, k_ref, v_ref):
    m = _rms(mem_ref[0], gmem_ref[...]).astype(BF16)
    k = jnp.dot(m, wmk_ref[...], preferred_element_type=F32)
    v = jnp.dot(m, wmv_ref[...], preferred_element_type=F32)
    g = gmkn_ref[...]
    for h in range(MEM_HEADS):
        sl = slice(h * MEM_HEAD_DIM, (h + 1) * MEM_HEAD_DIM)
        k_ref[0, :, sl] = _rms(k[:, sl], g).astype(BF16)
    v_ref[0] = v.astype(BF16)


def _memkv_call(mem, gmem, wmk, wmv, gmkn):
    B, M, D = mem.shape
    spec = pl.BlockSpec((1, M, D), lambda b: (b, 0, 0))
    os_ = jax.ShapeDtypeStruct((B, M, D), BF16)
    return pl.pallas_call(
        _memkv_kernel,
        grid=(B,),
        in_specs=[spec, _const_spec(gmem.shape), _const_spec(wmk.shape), _const_spec(wmv.shape),
                  _const_spec(gmkn.shape)],
        out_specs=[spec, spec],
        out_shape=[os_, os_],
        compiler_params=_cparams(("parallel",)),
        name="mem_kv",
    )(mem, gmem, wmk, wmv, gmkn)


def _mix_mem_kernel(x_ref, oa_ref, orn_ref, gao_ref, wout_ref, gxq_ref, wmq_ref, gmqn_ref, km_ref, vm_ref,
                    wmo_ref, o_ref):
    oa = _rms(oa_ref[0].astype(F32), gao_ref[...]).astype(BF16)
    mix = jnp.concatenate([oa, orn_ref[0]], axis=-1)
    x1 = x_ref[0] + jnp.dot(mix, wout_ref[...], preferred_element_type=F32)
    qm = _bdot(_rms(x1, gxq_ref[...]), wmq_ref[...])
    g = gmqn_ref[...]
    scale = MEM_HEAD_DIM ** -0.5

    def mem_scores(h):
        sl = slice(h * MEM_HEAD_DIM, (h + 1) * MEM_HEAD_DIM)
        qh = (_rms(qm[:, sl], g) * scale).astype(BF16)
        return lax.dot_general(qh, km_ref[0, :, sl], (((1,), (1,)), ((), ())), preferred_element_type=F32)

    heads = []
    s_next = mem_scores(0)
    for h in range(MEM_HEADS):
        sl = slice(h * MEM_HEAD_DIM, (h + 1) * MEM_HEAD_DIM)
        s = s_next
        if h + 1 < MEM_HEADS:
            s_next = mem_scores(h + 1)
        p = jnp.exp(s - jnp.max(s, axis=-1, keepdims=True))
        oh = jnp.dot(p.astype(BF16), vm_ref[0, :, sl], preferred_element_type=F32)
        heads.append((oh / jnp.sum(p, axis=-1, keepdims=True)).astype(BF16))
    o_ref[0] = x1 + jnp.dot(jnp.concatenate(heads, axis=-1), wmo_ref[...], preferred_element_type=F32)


def _mix_mem_call(x, oa, orn, gao, wout, gxq, wmq, gmqn, km, vm, wmo, tm):
    B, S, D = x.shape
    M = km.shape[1]
    xspec = pl.BlockSpec((1, tm, D), lambda b, i: (b, i, 0))
    hspec = pl.BlockSpec((1, tm, D // 2), lambda b, i: (b, i, 0))
    mspec = pl.BlockSpec((1, M, D), lambda b, i: (b, 0, 0))
    return pl.pallas_call(
        _mix_mem_kernel,
        grid=(B, S // tm),
        in_specs=[xspec, hspec, hspec, _const_spec(gao.shape), _const_spec(wout.shape), _const_spec(gxq.shape),
                  _const_spec(wmq.shape), _const_spec(gmqn.shape), mspec, mspec, _const_spec(wmo.shape)],
        out_specs=xspec,
        out_shape=jax.ShapeDtypeStruct((B, S, D), F32),
        compiler_params=_cparams(("parallel", "parallel")),
        name="out_proj_mem_attn",
    )(x, oa, orn, gao, wout, gxq, wmq, gmqn, km, vm, wmo)


def _route_t(lt):
    tm = lt.shape[1]
    srow = lax.broadcasted_iota(jnp.int32, (EXPERTS_PER_GROUP, tm), 0)
    glog = jnp.where(srow < N_GROUPS, lt[0:8], NEG)
    gmax = jnp.max(glog, axis=0, keepdims=True)
    p_g = 1.0 / jnp.sum(jnp.exp(glog - gmax), axis=0, keepdims=True)
    g_idx = jnp.min(jnp.where(glog == gmax, srow, 8), axis=0, keepdims=True)
    el = lt[ROUTER_OFF:ROUTER_OFF + EXPERTS_PER_GROUP]
    for gg in range(1, N_GROUPS):
        lo = ROUTER_OFF + gg * EXPERTS_PER_GROUP
        el = jnp.where(g_idx == gg, lt[lo:lo + EXPERTS_PER_GROUP], el)
    ee = jnp.exp(el - jnp.max(el, axis=0, keepdims=True))
    probs = ee / jnp.sum(ee, axis=0, keepdims=True)
    v1 = jnp.max(probs, axis=0, keepdims=True)
    i1 = jnp.min(jnp.where(probs == v1, srow, 8), axis=0, keepdims=True)
    rest = srow != i1
    v2 = jnp.max(jnp.where(rest, probs, -1.0), axis=0, keepdims=True)
    i2 = jnp.min(jnp.where(rest & (probs == v2), srow, 8), axis=0, keepdims=True)
    tot = v1 + v2
    gl = jnp.where(srow == i1, p_g * (v1 / tot), 0.0) + jnp.where(srow == i2, p_g * (v2 / tot), 0.0)
    return gl, g_idx


def _moe_kernel(x_ref, gffn_ref, wr_ref, br_ref, wgu_ref, wd_ref, o_ref, text_s, keyc_s, metar_s, cnt_s, *, ch):
    g = pl.program_id(1)
    tm, D = x_ref.shape

    @pl.when(g == 0)
    def _():
        x = x_ref[...]
        t = _rms(x, gffn_ref[...])
        thi = t.astype(BF16)
        tlo = (t - thi.astype(F32)).astype(BF16)
        l1 = jnp.dot(thi, wr_ref[...], preferred_element_type=F32)
        l2 = jnp.dot(tlo, wr_ref[:, 0:LANES], preferred_element_type=F32)
        logits = l1[:, 0:LANES] + l1[:, LANES:] + l2 + br_ref[...]
        gl, gid = _route_t(logits.T)
        srow = lax.broadcasted_iota(jnp.int32, (8, tm), 0)
        oh = srow == gid
        upper = lax.broadcasted_iota(jnp.int32, (tm, tm), 0) < lax.broadcasted_iota(jnp.int32, (tm, tm), 1)
        pref = jnp.dot(oh.astype(BF16), upper.astype(BF16), preferred_element_type=F32)
        rank = jnp.sum(jnp.where(oh, pref, 0.0), axis=0, keepdims=True)
        rg = jnp.where(srow == 0, rank, jnp.where(srow == 1, gid.astype(F32), 0.0))
        metar_s[...] = rg
        keyc_s[0] = jnp.broadcast_to(rank, (LANES, tm)).T
        keyc_s[1] = jnp.broadcast_to(gid.astype(F32), (LANES, tm)).T
        info = jnp.concatenate([gl, jnp.zeros((LANES - 8, tm), F32)], axis=0).T
        ghi = info.astype(BF16)
        text_s[:, 0:D] = thi
        text_s[:, D:D + LANES] = ghi
        text_s[:, D + LANES:] = (info - ghi.astype(F32)).astype(BF16)
        for gg in range(N_GROUPS):
            cnt_s[gg] = jnp.sum(oh[gg:gg + 1, :].astype(jnp.int32))
        o_ref[...] = x

    gf = g.astype(F32)
    nch = (cnt_s[g] + (ch - 1)) // ch
    key_row = jnp.where(metar_s[1:2, :] == gf, metar_s[0:1, :], -1.0).astype(jnp.int32)
    kc = jnp.where(keyc_s[1] == gf, keyc_s[0], -1.0).astype(jnp.int32)
    key_col = jnp.concatenate([kc] * (ch // LANES), axis=1)
    sub = lax.broadcasted_iota(jnp.int32, (ch, tm), 0)
    lan = lax.broadcasted_iota(jnp.int32, (tm, ch), 1)

    def chunk(c, carry):
        base = c * ch
        disp = (key_row - base == sub).astype(BF16)
        xc = jnp.dot(disp, text_s[...], preferred_element_type=F32)
        xb = xc[:, :D].astype(BF16)
        gat = xc[:, D:D + LANES] + xc[:, D + LANES:]
        y = jnp.zeros((ch, D), F32)
        ahead = 1
        hq = [jnp.dot(xb, wgu_ref[e], preferred_element_type=F32) for e in range(ahead)]
        for e in range(EXPERTS_PER_GROUP):
            h = hq.pop(0)
            if e + ahead < EXPERTS_PER_GROUP:
                hq.append(jnp.dot(xb, wgu_ref[e + ahead], preferred_element_type=F32))
            he = jax.nn.silu(h[:, :D_EXPERT]) * h[:, D_EXPERT:] * gat[:, e:e + 1]
            y = y + _bdot(he, wd_ref[e])
        comb = (key_col - base == lan).astype(BF16)
        o_ref[...] += jnp.dot(comb, y.astype(BF16), preferred_element_type=F32)
        return carry

    lax.fori_loop(0, nch, chunk, 0)


def _moe_call(x2, gffn, wr, br, wgu, wd, tm, ch):
    N, D = x2.shape
    xspec = pl.BlockSpec((tm, D), lambda i, g: (i, 0))
    epg = EXPERTS_PER_GROUP
    return pl.pallas_call(
        functools.partial(_moe_kernel, ch=ch),
        grid=(N // tm, N_GROUPS),
        in_specs=[xspec, _const_spec(gffn.shape), _const_spec(wr.shape), _const_spec(br.shape),
                  pl.BlockSpec((epg, D, 2 * D_EXPERT), lambda i, g: (g, 0, 0)),
                  pl.BlockSpec((epg, D_EXPERT, D), lambda i, g: (g, 0, 0))],
        out_specs=xspec,
        out_shape=jax.ShapeDtypeStruct((N, D), F32),
        scratch_shapes=[pltpu.VMEM((tm, D + 2 * LANES), BF16), pltpu.VMEM((2, tm, LANES), F32),
                        pltpu.VMEM((8, tm), F32), pltpu.SMEM((N_GROUPS,), jnp.int32)],
        compiler_params=_cparams(("parallel", "arbitrary")),
        name="hier_moe_grouped",
    )(x2, gffn, wr, br, wgu, wd)


def _pad_lanes(w, width):
    return jnp.pad(w, [(0, 0)] * (w.ndim - 1) + [(0, width - w.shape[-1])])


def _rope_cos_sin(S):
    pos = jnp.arange(S, dtype=F32)
    inv_freq = ROPE_BASE ** (-jnp.arange(0, MLA_ROPE, 2, dtype=F32) / MLA_ROPE)
    ang = pos[:, None] * inv_freq[None, :]
    return jnp.cos(ang), jnp.sin(ang)


def _rope_tables(cos, sin, gq, gk):
    S = cos.shape[0]
    n, half = MLA_NOPE, MLA_ROPE // 2
    z = lambda w: jnp.zeros((S, w), F32)
    gq_ext = jnp.concatenate([gq[:n], gq[n:n + half], gq[n + half:], gq[n + half:], gq[n:n + half]])
    scale = MLA_QK ** -0.5 * LOG2E
    tq = jnp.concatenate([jnp.ones((S, n), F32), cos, cos, -sin, sin], axis=1) * (scale * gq_ext)[None, :]
    gk_rope = jnp.concatenate([jnp.zeros((n,), F32), gk[n:], jnp.zeros((LANES - MLA_QK,), F32)])
    ck = jnp.concatenate([z(n), cos, cos, z(LANES - MLA_QK)], axis=1) * gk_rope[None, :]
    s1k = jnp.concatenate([z(n), -sin, z(LANES - n - half)], axis=1) * jnp.roll(gk_rope, -half)[None, :]
    s2k = jnp.concatenate([z(n + half), sin, z(LANES - MLA_QK)], axis=1) * jnp.roll(gk_rope, half)[None, :]
    return tq, ck, s1k, s2k


def _block_diag(w):
    n, c, d = w.shape
    eye = jnp.eye(n, dtype=w.dtype)
    return (eye[:, None, :, None] * w[:, :, None, :]).reshape(n * c, n * d)


def kernel(x, mem, g_mix, w_in, g_cq, w_uq, g_ckv, w_ukv, g_qn, g_kn, conv_w, conv_b, w_rg, b_rg, w_ig, b_ig,
           lam, g_attn_out, g_rnn_out, w_out, g_xq, g_mem, w_mq, w_mk, w_mv, g_mqn, g_mkn, w_mo, g_ffn,
           w_group, b_group, w_expert, b_expert, w_e_gate, w_e_up, w_e_down):
    B, S, D = x.shape
    H = MLA_HEADS
    tm = min(1024, S)
    tk = min(256, S)
    tq = min(1024, S)
    row = lambda a: a.reshape(1, -1)
    cos, sin = _rope_cos_sin(S)
    for l in range(g_mix.shape[0]):
        wi = w_in[l]
        c0, c1, c2 = Q_LORA + KV_LORA, Q_LORA + KV_LORA + MLA_ROPE, Q_LORA + KV_LORA + MLA_ROPE + RNN_WIDTH
        zc = lambda n: jnp.zeros((D, n), wi.dtype)
        win = jnp.concatenate([wi[:, :c0], zc(MLA_NOPE), wi[:, c0:c1], zc(LANES - MLA_QK), wi[:, c1:c2], wi[:, c2:]],
                              axis=1).astype(BF16)
        wq = w_uq[l].reshape(Q_LORA, H, MLA_QK)
        n, half = MLA_NOPE, MLA_ROPE // 2
        wuq = jnp.concatenate([wq, wq[:, :, n + half:], wq[:, :, n:n + half]], axis=-1).reshape(Q_LORA, H * LANES)
        wuq = wuq.astype(BF16)
        wkv = w_ukv[l].reshape(KV_LORA, H, MLA_NOPE + MLA_V)
        wukv = jnp.concatenate([_pad_lanes(wkv[:, :, :MLA_NOPE], LANES).reshape(KV_LORA, H * LANES),
                                _pad_lanes(wkv[:, :, MLA_NOPE:], LANES).reshape(KV_LORA, H * LANES)],
                               axis=1).astype(BF16)
        gkn = _pad_lanes(row(g_kn[l][:MLA_NOPE]), LANES)
        tabs = _rope_tables(cos, sin, g_qn[l], g_kn[l])
        wrg = _block_diag(w_rg[l]).astype(BF16)
        wig = _block_diag(w_ig[l]).astype(BF16)
        gpad = ROUTER_OFF - N_GROUPS
        wr32 = _pad_lanes(jnp.concatenate([_pad_lanes(w_group[l], ROUTER_OFF), w_expert[l]], axis=1), LANES)
        wr_hi = wr32.astype(BF16)
        wr = jnp.concatenate([wr_hi, (wr32 - wr_hi.astype(F32)).astype(BF16)], axis=1)
        br = _pad_lanes(row(jnp.concatenate([b_group[l], jnp.zeros((gpad,), F32), b_expert[l]])), LANES)
        wgu = jnp.concatenate([w_e_gate[l], w_e_up[l]], axis=-1).astype(BF16)
        wd = w_e_down[l].astype(BF16)

        q, k, v, ug, ux = _proj_call(x, row(g_mix[l]), win, row(g_cq[l]), wuq, row(g_ckv[l]), wukv, gkn, tabs, tm)
        o_rnn = _rglru_call(ug, ux, conv_w[l], row(conv_b[l]), wrg, row(b_rg[l]), wig, row(b_ig[l]), row(lam[l]),
                            row(g_rnn_out[l]), tm)
        o_attn = _attn_call(q, k, v, tq, tk, 4)
        km, vm = _memkv_call(mem, row(g_mem[l]), w_mk[l].astype(BF16), w_mv[l].astype(BF16), row(g_mkn[l]))
        x2 = _mix_mem_call(x, o_attn, o_rnn, row(g_attn_out[l]), w_out[l].astype(BF16), row(g_xq[l]),
                           w_mq[l].astype(BF16), row(g_mqn[l]), km, vm, w_mo[l].astype(BF16), tm)
        x = _moe_call(x2.reshape(B * S, D), row(g_ffn[l]), wr, br, wgu, wd, min(1024, B * S), MOE_CHUNK).reshape(B, S, D)
    return x
```
